```python
import jax, jax.numpy as jnp
from jax import lax
import numpy as np

D_MODEL = 2048
BATCH = 4
SEQ = 2048
DEPTH = 4
DEC_BATCH = 8
DEC_SEQ = 8
PAST_LEN = 16384
PAGE_SIZE = 128

N_EVEN = (DEPTH + 1) // 2
N_ODD = DEPTH // 2
W_BRANCH = D_MODEL // 2
A_GROUPS = 8
A_CHUNK = 128
A_GDIM = W_BRANCH // A_GROUPS
B_HEADS = 4
B_DK = W_BRANCH // B_HEADS
B_DV = B_DK
B_CHUNK = 128
ROPE_BASE = 10000.0
POOL_WINDOWS = (2, 4, 8, 16)
C_GROUPS = 4
C_GDIM = W_BRANCH // C_GROUPS
POOL_BUF = 15
D_HEADS = 8
D_DH = W_BRANCH // D_HEADS
Q_BLOCK = 128
FORGET_BIAS = 3.0
EPS = 1e-6
IN_EVEN = 7 * W_BRANCH
IN_ODD = 6 * W_BRANCH + D_HEADS

kernel_name = 'hybrid_sgu_retention_pool_fox_decoder_step'


def rmsnorm(x, g):
    xf = x.astype(jnp.float32)
    y = xf * lax.rsqrt(jnp.mean(xf * xf, axis=-1, keepdims=True) + EPS)
    return (y * g.astype(jnp.float32)).astype(x.dtype)


def layernorm(x, g, b):
    xf = x.astype(jnp.float32)
    xc = xf - jnp.mean(xf, axis=-1, keepdims=True)
    y = xc * lax.rsqrt(jnp.mean(xc * xc, axis=-1, keepdims=True) + EPS)
    return (y * g.astype(jnp.float32) + b.astype(jnp.float32)).astype(x.dtype)


def rope(x, pos):
    half = x.shape[-1] // 2
    inv = ROPE_BASE ** (-jnp.arange(half, dtype=jnp.float32) / half)
    ang = pos.astype(jnp.float32)[:, None] * inv[None, :]
    cos = jnp.cos(ang)[None, :, None, :]
    sin = jnp.sin(ang)[None, :, None, :]
    xf = x.astype(jnp.float32)
    x1, x2 = xf[..., :half], xf[..., half:]
    return jnp.concatenate([x1 * cos - x2 * sin, x1 * sin + x2 * cos], axis=-1)


def modulate_in(x, c, g_pre, w_ada, b_ada):
    mod = jnp.dot(jax.nn.silu(c), w_ada) + b_ada
    shift, scale, gate = jnp.split(mod[:, None, :], 3, axis=-1)
    h = rmsnorm(x, g_pre) * (1 + scale) + shift
    return h, gate


def chunk_spatial_gate(u, v, w_s, b_s):
    Bn, L, _ = v.shape
    n = -(-L // A_CHUNK)
    pad = n * A_CHUNK - L
    vp = jnp.pad(v, ((0, 0), (0, pad), (0, 0))).reshape(Bn, n, A_CHUNK, A_GROUPS, A_GDIM)
    causal = jnp.tril(jnp.ones((A_CHUNK, A_CHUNK), dtype=bool))
    wm = jnp.where(causal[None], w_s, 0).astype(v.dtype)
    mixed = jnp.einsum('gts,bnsgc->bntgc', wm, vp) + jnp.swapaxes(b_s, 0, 1)[None, None, :, :, None]
    mixed = mixed.reshape(Bn, n * A_CHUNK, W_BRANCH)[:, :L]
    return u * mixed.astype(u.dtype)


def retention(q, k, v, s0):
    Bn, L = q.shape[:2]
    log_gamma = jnp.log(1.0 - 2.0 ** (-5.0 - jnp.arange(B_HEADS, dtype=jnp.float32)))
    C = B_CHUNK if L % B_CHUNK == 0 else L
    n = L // C
    t = jnp.arange(C, dtype=jnp.float32)
    diff = t[:, None] - t[None, :]
    dmask = jnp.where(diff >= 0, jnp.exp(log_gamma[:, None, None] * jnp.maximum(diff, 0.0)), 0.0)
    q_dec = jnp.exp(log_gamma[None, :] * (t + 1.0)[:, None])[None, :, :, None]
    k_dec = jnp.exp(log_gamma[None, :] * (C - 1.0 - t)[:, None])[None, :, :, None]
    c_dec = jnp.exp(log_gamma * C)[None, :, None, None]

    def to_chunks(a):
        return jnp.moveaxis(a.astype(jnp.float32).reshape(Bn, n, C, *a.shape[2:]), 1, 0)

    def step(s, inp):
        qi, ki, vi = inp
        inner = jnp.einsum('bthd,bshd->bhts', qi, ki) * dmask[None]
        o = jnp.einsum('bhts,bshe->bthe', inner, vi) + jnp.einsum('bthd,bhde->bthe', qi, s) * q_dec
        s = s * c_dec + jnp.einsum('bshd,bshe->bhde', ki * k_dec, vi)
        return s, o

    s_fin, o = lax.scan(step, s0.astype(jnp.float32), (to_chunks(q), to_chunks(k), to_chunks(v)))
    o = jnp.moveaxis(o, 0, 1).reshape(Bn, L, B_HEADS, B_DV)
    return o, s_fin


def even_mixer(h, pos, s0, w_in, w_out, ln_g, ln_b, w_s, b_s, ret_g):
    Bn, L, _ = h.shape
    z = jnp.dot(h, w_in)
    a_u, a_v, a_g, r_q, r_k, r_v, r_g = jnp.split(z, 7, axis=-1)
    a_vn = layernorm(a_v, ln_g, ln_b)
    a_out = chunk_spatial_gate(a_u, a_vn, w_s, b_s) * jax.nn.silu(a_g)
    heads = lambda a: a.reshape(Bn, L, B_HEADS, B_DK)
    q = rope(heads(r_q), pos)
    k = rope(heads(r_k), pos) * (B_DK ** -0.5)
    o, s_new = retention(q, k, heads(r_v), s0)
    o = rmsnorm(o, ret_g.reshape(B_HEADS, B_DV)).reshape(Bn, L, W_BRANCH)
    r_out = o.astype(h.dtype) * jax.nn.silu(r_g)
    y = jnp.dot(jnp.concatenate([a_out, r_out], axis=-1), w_out)
    return y, s_new, a_vn


def odd_project(h, w_in, b_f):
    Bn, L, _ = h.shape
    z = jnp.dot(h, w_in)
    W = W_BRANCH
    hd = lambda a: a.reshape(Bn, L, D_HEADS, D_DH)
    logf = jax.nn.log_sigmoid((z[..., 6 * W:] + b_f).astype(jnp.float32))
    return (z[..., :W], z[..., W:2 * W], hd(z[..., 2 * W:3 * W]), hd(z[..., 3 * W:4 * W]),
            hd(z[..., 4 * W:5 * W]), z[..., 5 * W:6 * W], logf)


def multiscale_pool(u_ext, n_prefix, pos0):
    Bn, Le, _ = u_ext.shape
    L = Le - n_prefix
    uf = u_ext.astype(jnp.float32)
    cs = jnp.concatenate([jnp.zeros((Bn, 1, W_BRANCH), jnp.float32), jnp.cumsum(uf, axis=1)], axis=1)
    hi = cs[:, n_prefix + 1:]
    pos = pos0 + jnp.arange(L)
    outs = []
    for gi, w in enumerate(POOL_WINDOWS):
        sl = slice(gi * C_GDIM, (gi + 1) * C_GDIM)
        lo_idx = np.clip(np.arange(n_prefix + 1 - w, Le + 1 - w), 0, None)
        lo = jnp.take(cs[:, :, sl], lo_idx, axis=1)
        cnt = jnp.minimum(pos + 1, w).astype(jnp.float32)
        outs.append((hi[:, :, sl] - lo) / cnt[None, :, None] - uf[:, n_prefix:, sl])
    return jnp.stack(outs, axis=2)


def pool_branch(u_ext, n_prefix, pos0, c_g, w_pool, pool_scale):
    pooled = multiscale_pool(u_ext, n_prefix, pos0)
    Bn, L = pooled.shape[:2]
    mixed = jnp.einsum('blgc,gcd->blgd', pooled, w_pool.astype(jnp.float32)).reshape(Bn, L, W_BRANCH)
    return (mixed * pool_scale.astype(jnp.float32)).astype(c_g.dtype) * jax.nn.silu(c_g)


def forgetting_attn_prompt(q, k, v, logf):
    Bn, S = q.shape[:2]
    scale = D_DH ** -0.5
    kf = k.astype(jnp.float32)
    vf = v.astype(jnp.float32)
    f_cum = jnp.swapaxes(jnp.cumsum(logf, axis=1), 1, 2)
    kpos = jnp.arange(S)

    def block(i):
        q0 = i * Q_BLOCK
        qb = lax.dynamic_slice_in_dim(q, q0, Q_BLOCK, axis=1).astype(jnp.float32)
        fq = lax.dynamic_slice_in_dim(f_cum, q0, Q_BLOCK, axis=2)
        logits = jnp.einsum('bqhd,bkhd->bhqk', qb, kf) * scale + fq[..., :, None] - f_cum[..., None, :]
        qpos = q0 + jnp.arange(Q_BLOCK)
        logits = jnp.where(kpos[None, :] <= qpos[:, None], logits, -jnp.inf)
        p = jax.nn.softmax(logits, axis=-1)
        return jnp.einsum('bhqk,bkhd->bqhd', p, vf)

    o = lax.map(block, jnp.arange(S // Q_BLOCK))
    return jnp.moveaxis(o, 0, 1).reshape(Bn, S, D_HEADS, D_DH)


def forgetting_attn_sample(q, k, v, logf, k_past, v_past, logf_past):
    scale = D_DH ** -0.5
    L = q.shape[1]
    P = k_past.shape[1]
    qf = q.astype(jnp.float32)
    lfp = logf_past.astype(jnp.float32)
    g_past = jnp.swapaxes(lax.cumsum(lfp, axis=1, reverse=True) - lfp, 1, 2)
    c_new = jnp.swapaxes(jnp.cumsum(logf, axis=1), 1, 2)
    lp = jnp.einsum('bqhd,bkhd->bhqk', qf, k_past.astype(jnp.float32)) * scale + c_new[..., :, None] + g_past[..., None, :]
    ln = jnp.einsum('bqhd,bkhd->bhqk', qf, k.astype(jnp.float32)) * scale + c_new[..., :, None] - c_new[..., None, :]
    ln = jnp.where(jnp.tril(jnp.ones((L, L), dtype=bool)), ln, -jnp.inf)
    p = jax.nn.softmax(jnp.concatenate([lp, ln], axis=-1), axis=-1)
    return (jnp.einsum('bhqk,bkhd->bqhd', p[..., :P], v_past.astype(jnp.float32))
            + jnp.einsum('bhqk,bkhd->bqhd', p[..., P:], v.astype(jnp.float32)))


def setup_inputs(seed: int = 0) -> dict:
    key = jax.random.key(seed)
    ks = jax.random.split(key, 32)
    f32 = jnp.float32
    nrm = lambda k, shape, s: jax.random.normal(k, shape, f32) * s
    n_pages = PAST_LEN // PAGE_SIZE
    n_phys = (DEC_BATCH * n_pages * 5) // 4
    page_table = jax.random.permutation(ks[0], n_phys)[:DEC_BATCH * n_pages].reshape(DEC_BATCH, n_pages).astype(jnp.int32)
    return {
        'x_prompt': nrm(ks[1], (BATCH, SEQ, D_MODEL), 1.0),
        'x_sample': nrm(ks[2], (DEC_BATCH, DEC_SEQ, D_MODEL), 1.0),
        'c_prompt': nrm(ks[3], (BATCH, D_MODEL), 1.0),
        'c_sample': nrm(ks[4], (DEC_BATCH, D_MODEL), 1.0),
        'state_ret': nrm(ks[5], (N_EVEN, DEC_BATCH, B_HEADS, B_DK, B_DV), 0.5),
        'state_pool': nrm(ks[6], (N_ODD, DEC_BATCH, POOL_BUF, W_BRANCH), 1.0),
        'cache_k': nrm(ks[7], (N_ODD, n_phys, PAGE_SIZE, D_HEADS, D_DH), 1.0),
        'cache_v': nrm(ks[8], (N_ODD, n_phys, PAGE_SIZE, D_HEADS, D_DH), 1.0),
        'cache_logf': jax.nn.log_sigmoid(FORGET_BIAS + nrm(ks[9], (N_ODD, n_phys, PAGE_SIZE, D_HEADS), 1.0)),
        'page_table': page_table,
        'g_pre': 1.0 + nrm(ks[10], (DEPTH, D_MODEL), 0.01),
        'g_post': 1.0 + nrm(ks[11], (DEPTH, D_MODEL), 0.01),
        'w_ada': nrm(ks[12], (DEPTH, D_MODEL, 3 * D_MODEL), 0.5 * D_MODEL ** -0.5),
        'b_ada': nrm(ks[13], (DEPTH, 3 * D_MODEL), 0.01),
        'w_in_even': nrm(ks[14], (N_EVEN, D_MODEL, IN_EVEN), D_MODEL ** -0.5),
        'w_out_even': nrm(ks[15], (N_EVEN, 2 * W_BRANCH, D_MODEL), (2 * W_BRANCH) ** -0.5),
        'ln_a_g': 1.0 + nrm(ks[16], (N_EVEN, W_BRANCH), 0.01),
        'ln_a_b': nrm(ks[17], (N_EVEN, W_BRANCH), 0.01),
        'w_s': nrm(ks[18], (N_EVEN, A_GROUPS, A_CHUNK, A_CHUNK), A_CHUNK ** -0.5),
        'b_s': 1.0 + nrm(ks[19], (N_EVEN, A_GROUPS, A_CHUNK), 0.1),
        'ret_g': 1.0 + nrm(ks[20], (N_EVEN, W_BRANCH), 0.01),
        'w_in_odd': nrm(ks[21], (N_ODD, D_MODEL, IN_ODD), D_MODEL ** -0.5),
        'b_f': FORGET_BIAS + nrm(ks[22], (N_ODD, D_HEADS), 0.1),
        'w_out_odd': nrm(ks[23], (N_ODD, 2 * W_BRANCH, D_MODEL), (2 * W_BRANCH) ** -0.5),
        'w_pool': nrm(ks[24], (N_ODD, C_GROUPS, C_GDIM, C_GDIM), C_GDIM ** -0.5),
        'pool_scale': 1.0 + nrm(ks[25], (N_ODD, W_BRANCH), 0.01),
    }


def reference(x_prompt, x_sample, c_prompt, c_sample, state_ret, state_pool, cache_k, cache_v, cache_logf,
              page_table, g_pre, g_post, w_ada, b_ada, w_in_even, w_out_even, ln_a_g, ln_a_b, w_s, b_s,
              ret_g, w_in_odd, b_f, w_out_odd, w_pool, pool_scale):
    Bp, S = x_prompt.shape[:2]
    Bd, L = x_sample.shape[:2]
    n_past = page_table.shape[1] * PAGE_SIZE
    pos_p = jnp.arange(S)
    pos_s = n_past + jnp.arange(L)
    xp, xs = x_prompt, x_sample
    ret_p, ret_s, gv_s = [], [], []
    pool_p, pool_s = [], []
    kp_l, vp_l, lfp_l, ks_l, vs_l, lfs_l = [], [], [], [], [], []
    for l in range(DEPTH):
        j = l // 2
        hp, gate_p = modulate_in(xp, c_prompt, g_pre[l], w_ada[l], b_ada[l])
        hs, gate_s = modulate_in(xs, c_sample, g_pre[l], w_ada[l], b_ada[l])
        if l % 2 == 0:
            s0_p = jnp.zeros((Bp, B_HEADS, B_DK, B_DV), jnp.float32)
            yp, sp, _ = even_mixer(hp, pos_p, s0_p, w_in_even[j], w_out_even[j], ln_a_g[j], ln_a_b[j], w_s[j], b_s[j], ret_g[j])
            ys, ss, vs_rows = even_mixer(hs, pos_s, state_ret[j], w_in_even[j], w_out_even[j], ln_a_g[j], ln_a_b[j], w_s[j], b_s[j], ret_g[j])
            ret_p.append(sp)
            ret_s.append(ss)
            gv_s.append(vs_rows)
        else:
            c_in, c_g, q, k, v, d_g, logf = odd_project(hp, w_in_odd[j], b_f[j])
            c_out = pool_branch(c_in, 0, 0, c_g, w_pool[j], pool_scale[j])
            d_out = forgetting_attn_prompt(q, k, v, logf).reshape(Bp, S, W_BRANCH).astype(hp.dtype) * jax.nn.silu(d_g)
            yp = jnp.dot(jnp.concatenate([c_out, d_out], axis=-1), w_out_odd[j])
            pool_p.append(c_in[:, -POOL_BUF:])
            kp_l.append(k)
            vp_l.append(v)
            lfp_l.append(logf)
            c_in, c_g, q, k, v, d_g, logf = odd_project(hs, w_in_odd[j], b_f[j])
            u_ext = jnp.concatenate([state_pool[j].astype(c_in.dtype), c_in], axis=1)
            c_out = pool_branch(u_ext, POOL_BUF, n_past, c_g, w_pool[j], pool_scale[j])
            k_past = cache_k[j, page_table].reshape(Bd, n_past, D_HEADS, D_DH)
            v_past = cache_v[j, page_table].reshape(Bd, n_past, D_HEADS, D_DH)
            lf_past = cache_logf[j, page_table].reshape(Bd, n_past, D_HEADS)
            d_out = forgetting_attn_sample(q, k, v, logf, k_past, v_past, lf_past).reshape(Bd, L, W_BRANCH).astype(hs.dtype) * jax.nn.silu(d_g)
            ys = jnp.dot(jnp.concatenate([c_out, d_out], axis=-1), w_out_odd[j])
            pool_s.append(u_ext[:, -POOL_BUF:])
            ks_l.append(k)
            vs_l.append(v)
            lfs_l.append(logf)
        xp = xp + gate_p * rmsnorm(yp, g_post[l])
        xs = xs + gate_s * rmsnorm(ys, g_post[l])
    return (xp, xs, jnp.stack(ret_p), jnp.stack(ret_s), jnp.stack(gv_s), jnp.stack(pool_p), jnp.stack(pool_s),
            jnp.stack(kp_l), jnp.stack(vp_l), jnp.stack(lfp_l), jnp.stack(ks_l), jnp.stack(vs_l), jnp.stack(lfs_l))
```

```python
import functools

import numpy as np
import jax
import jax.numpy as jnp
from jax import lax
from jax.experimental import pallas as pl
from jax.experimental.pallas import tpu as pltpu

F32 = jnp.float32
BF16 = jnp.bfloat16

EPS = 1e-6
ROPE_BASE = 10000.0
CHUNK = 128
A_GROUPS = 8
B_HEADS = 4
POOL_WINDOWS = (2, 4, 8, 16)
POOL_BUF = 15
D_HEADS = 8
PAGE_SIZE = 128
LANES = 128
SUBLANES = 8
NEG_BIG = -1e30
VMEM_LIMIT = 56 * 1024 * 1024


def _cparams(*sem):
    return pltpu.CompilerParams(dimension_semantics=sem, vmem_limit_bytes=VMEM_LIMIT)


def _silu(x):
    return x * jax.nn.sigmoid(x)


def _log_sigmoid(x):
    return jnp.minimum(x, 0.0) - jnp.log1p(jnp.exp(-jnp.abs(x)))


def _dot(a, b):
    return jnp.dot(a, b, preferred_element_type=F32)


def _dot_nt(a, b):
    return lax.dot_general(a, b, (((1,), (1,)), ((), ())), preferred_element_type=F32)


def _dot_tn(a, b):
    return lax.dot_general(a, b, (((0,), (0,)), ((), ())), preferred_element_type=F32)


def _ada_kernel(c_ref, w_ref, b_ref, o_ref):
    a = _silu(c_ref[...]).astype(BF16)
    o_ref[0] = _dot(a, w_ref[0].astype(BF16)) + b_ref[0]


def _ada_mod(c_all, w_ada, b_ada, tn=1024):
    depth, d, n = w_ada.shape
    r = c_all.shape[0]
    return pl.pallas_call(
        _ada_kernel,
        grid=(depth, n // tn),
        in_specs=[
            pl.BlockSpec((r, d), lambda l, j: (0, 0)),
            pl.BlockSpec((1, d, tn), lambda l, j: (l, 0, j)),
            pl.BlockSpec((1, 1, tn), lambda l, j: (l, 0, j)),
        ],
        out_specs=pl.BlockSpec((1, r, tn), lambda l, j: (l, 0, j)),
        out_shape=jax.ShapeDtypeStruct((depth, r, n), F32),
        compiler_params=_cparams("arbitrary", "arbitrary"),
        name="ada_mod",
    )(c_all, w_ada, b_ada.reshape(depth, 1, n))


def _modnorm_to(h_scr, x_ref, sc_ref, sh_ref, g_ref):
    x = x_ref[...]
    y = x * lax.rsqrt(jnp.mean(x * x, axis=-1, keepdims=True) + EPS) * g_ref[...]
    h_scr[...] = (y * (1.0 + sc_ref[0]) + sh_ref[0]).astype(BF16)


def _inproj_even_kernel(x_ref, sc_ref, sh_ref, g_ref, w_ref, z_ref, h_scr):
    @pl.when(pl.program_id(1) == 0)
    def _():
        _modnorm_to(h_scr, x_ref, sc_ref, sh_ref, g_ref)

    z_ref[...] = _dot(h_scr[...], w_ref[...])


def _inproj_odd_kernel(x_ref, sc_ref, sh_ref, g_ref, w_ref, wf_ref, bf_ref, *rest, tiles_per_group):
    z_ref, k_ref, v_ref, lf_ref, h_scr = rest[-5:]
    j = pl.program_id(1)

    @pl.when(j == 0)
    def _():
        _modnorm_to(h_scr, x_ref, sc_ref, sh_ref, g_ref)
        lf_ref[...] = _log_sigmoid(_dot(h_scr[...], wf_ref[...]) + bf_ref[...])

    z = _dot(h_scr[...], w_ref[...])
    group = j // tiles_per_group

    @pl.when(jnp.logical_or(group < 3, group == 5))
    def _():
        z_ref[...] = z

    @pl.when(group == 3)
    def _():
        k_ref[0] = z

    @pl.when(group == 4)
    def _():
        v_ref[0] = z


def _mod_specs(per_row, rows_per_mod, tm, d):
    if per_row:
        return pl.BlockSpec((1, tm, d), lambda i, j: (0, i, 0))
    return pl.BlockSpec((1, 1, d), lambda i, j: ((i * tm) // rows_per_mod, 0, 0))


def _inproj_even(x2d, scale, shift, g, w_bf, *, per_row, rows_per_mod, tm, tn=512):
    rows, d = x2d.shape
    n = w_bf.shape[1]
    return pl.pallas_call(
        _inproj_even_kernel,
        grid=(rows // tm, n // tn),
        in_specs=[
            pl.BlockSpec((tm, d), lambda i, j: (i, 0)),
            _mod_specs(per_row, rows_per_mod, tm, d),
            _mod_specs(per_row, rows_per_mod, tm, d),
            pl.BlockSpec((1, d), lambda i, j: (0, 0)),
            pl.BlockSpec((d, tn), lambda i, j: (0, j)),
        ],
        out_specs=pl.BlockSpec((tm, tn), lambda i, j: (i, j)),
        out_shape=jax.ShapeDtypeStruct((rows, n), F32),
        scratch_shapes=[pltpu.VMEM((tm, d), BF16)],
        compiler_params=_cparams("arbitrary", "arbitrary"),
        name="inproj_even",
    )(x2d, scale, shift, g, w_bf)


def _inproj_odd(x2d, scale, shift, g, w_bf, wf_bf, bf, kv_prev, *, layer_slot, n_slots,
                per_row, rows_per_mod, tm, tn=512):
    rows, d = x2d.shape
    w = w_bf.shape[1] // 6
    tpg = w // tn
    n_tiles = 6 * tpg

    def zmap(i, j):
        return (i, j - jnp.clip(j - (3 * tpg - 1), 0, 2 * tpg))

    def kmap(i, j):
        return (layer_slot, i, jnp.clip(j - 3 * tpg, 0, tpg - 1))

    def vmap(i, j):
        return (layer_slot, i, jnp.clip(j - 4 * tpg, 0, tpg - 1))

    in_specs = [
        pl.BlockSpec((tm, d), lambda i, j: (i, 0)),
        _mod_specs(per_row, rows_per_mod, tm, d),
        _mod_specs(per_row, rows_per_mod, tm, d),
        pl.BlockSpec((1, d), lambda i, j: (0, 0)),
        pl.BlockSpec((d, tn), lambda i, j: (0, j)),
        pl.BlockSpec((d, LANES), lambda i, j: (0, 0)),
        pl.BlockSpec((1, LANES), lambda i, j: (0, 0)),
    ]
    args = [x2d, scale, shift, g, w_bf, wf_bf, bf]
    aliases = {}
    if kv_prev is not None:
        in_specs += [pl.BlockSpec(memory_space=pl.ANY)] * 2
        args += list(kv_prev)
        aliases = {7: 1, 8: 2}
    kv_shape = jax.ShapeDtypeStruct((n_slots, rows, w), F32)
    return pl.pallas_call(
        functools.partial(_inproj_odd_kernel, tiles_per_group=tpg),
        grid=(rows // tm, n_tiles),
        in_specs=in_specs,
        out_specs=[
            pl.BlockSpec((tm, tn), zmap),
            pl.BlockSpec((1, tm, tn), kmap),
            pl.BlockSpec((1, tm, tn), vmap),
            pl.BlockSpec((tm, LANES), lambda i, j: (i, 0)),
        ],
        out_shape=[
            jax.ShapeDtypeStruct((rows, 4 * w), F32),
            kv_shape,
            kv_shape,
            jax.ShapeDtypeStruct((rows, LANES), F32),
        ],
        scratch_shapes=[pltpu.VMEM((tm, d), BF16)],
        input_output_aliases=aliases,
        compiler_params=_cparams("arbitrary", "arbitrary"),
        name="inproj_odd",
    )(*args)


def _outproj_kernel(a_ref, b_ref, x_ref, gate_ref, g_ref, w1_ref, w2_ref, o_ref):
    y = _dot(a_ref[...].astype(BF16), w1_ref[...]) + _dot(b_ref[...].astype(BF16), w2_ref[...])
    yn = y * lax.rsqrt(jnp.mean(y * y, axis=-1, keepdims=True) + EPS) * g_ref[...]
    o_ref[...] = x_ref[...] + gate_ref[0] * yn


def _outproj(a, b, x2d, gate, g, w_bf, *, per_row, rows_per_mod, tm):
    rows, d = x2d.shape
    w = a.shape[1]
    if per_row:
        gate_spec = pl.BlockSpec((1, tm, d), lambda i: (0, i, 0))
    else:
        gate_spec = pl.BlockSpec((1, 1, d), lambda i: ((i * tm) // rows_per_mod, 0, 0))
    return pl.pallas_call(
        _outproj_kernel,
        grid=(rows // tm,),
        in_specs=[
            pl.BlockSpec((tm, w), lambda i: (i, 0)),
            pl.BlockSpec((tm, w), lambda i: (i, 0)),
            pl.BlockSpec((tm, d), lambda i: (i, 0)),
            gate_spec,
            pl.BlockSpec((1, d), lambda i: (0, 0)),
            pl.BlockSpec((w, d), lambda i: (0, 0)),
            pl.BlockSpec((w, d), lambda i: (1, 0)),
        ],
        out_specs=pl.BlockSpec((tm, d), lambda i: (i, 0)),
        out_shape=jax.ShapeDtypeStruct((rows, d), F32),
        compiler_params=_cparams("arbitrary"),
        name="outproj",
    )(a, b, x2d, gate, g, w_bf, w_bf)


def _retention_tables(c_len, pos0, n_rows):
    lg = np.log(1.0 - 2.0 ** (-5.0 - np.arange(B_HEADS, dtype=np.float32))).astype(np.float32)
    t = np.arange(CHUNK, dtype=np.float32)
    diff = t[:, None] - t[None, :]
    dmask = np.where(diff >= 0, np.exp(lg[:, None, None] * np.maximum(diff, 0.0)), 0.0).astype(np.float32)
    qdec = np.exp(lg[None, :] * (t + 1.0)[:, None]).astype(np.float32)
    kdec = np.exp(lg[None, :] * np.maximum(c_len - 1.0 - t, 0.0)[:, None]).astype(np.float32)
    cdec = tuple(float(v) for v in np.exp(lg * np.float32(c_len)).astype(np.float32))
    half = LANES
    inv = (ROPE_BASE ** (-np.arange(half, dtype=np.float32) / half)).astype(np.float32)
    pos = (pos0 + np.arange(n_rows)).astype(np.float32)
    ang = pos[:, None] * inv[None, :]
    return dmask, qdec, kdec, cdec, ang


def _even_mixer_kernel(u_ref, v_ref, ga_ref, q_ref, k_ref, vv_ref, gr_ref,
                       lng_ref, lnb_ref, ws_ref, bst_ref, retg_ref, cos_ref, sin_ref,
                       dmask_ref, qdec_ref, kdec_ref, s0_ref, *rest, c_in, cdec, emit_vn):
    if emit_vn:
        ao_ref, ro_ref, so_ref, vn_ref, s_scr = rest
    else:
        ao_ref, ro_ref, so_ref, s_scr = rest
    c = pl.program_id(1)

    @pl.when(c == 0)
    def _():
        s_scr[...] = s0_ref[0]

    def ld(ref):
        x = ref[...]
        if c_in == CHUNK:
            return x
        return jnp.concatenate([x, jnp.zeros((CHUNK - c_in, x.shape[1]), F32)], axis=0)

    v = ld(v_ref)
    xc = v - jnp.mean(v, axis=-1, keepdims=True)
    vn = xc * lax.rsqrt(jnp.mean(xc * xc, axis=-1, keepdims=True) + EPS) * lng_ref[...] + lnb_ref[...]
    if emit_vn:
        vn_ref[...] = vn[:c_in]
    vnb = vn.astype(BF16)
    u = ld(u_ref)
    ga = ld(ga_ref)
    causal = (lax.broadcasted_iota(jnp.int32, (CHUNK, CHUNK), 0)
              >= lax.broadcasted_iota(jnp.int32, (CHUNK, CHUNK), 1))
    gw = u.shape[1] // A_GROUPS
    for g in range(A_GROUPS):
        sl = slice(g * gw, (g + 1) * gw)
        wm = jnp.where(causal, ws_ref[g], 0.0).astype(BF16)
        mixed = _dot(wm, vnb[:, sl]) + bst_ref[:, g:g + 1]
        ao_ref[:, sl] = (u[:, sl] * mixed * _silu(ga[:, sl]))[:c_in].astype(ao_ref.dtype)

    cos = cos_ref[...]
    sin = sin_ref[...]
    q = ld(q_ref)
    k = ld(k_ref)
    vv = ld(vv_ref)
    gr = ld(gr_ref)
    dk = q.shape[1] // B_HEADS
    half = dk // 2
    for h in range(B_HEADS):
        sl = slice(h * dk, (h + 1) * dk)
        q1, q2 = q[:, h * dk:h * dk + half], q[:, h * dk + half:(h + 1) * dk]
        k1, k2 = k[:, h * dk:h * dk + half], k[:, h * dk + half:(h + 1) * dk]
        qr = jnp.concatenate([q1 * cos - q2 * sin, q1 * sin + q2 * cos], axis=1)
        kr = jnp.concatenate([k1 * cos - k2 * sin, k1 * sin + k2 * cos], axis=1) * (dk ** -0.5)
        qrb = qr.astype(BF16)
        vb = vv[:, sl].astype(BF16)
        inner = _dot_nt(qrb, kr.astype(BF16)) * dmask_ref[h]
        s = s_scr[h]
        o = _dot(inner.astype(BF16), vb) + _dot(qrb, s.astype(BF16)) * qdec_ref[:, h:h + 1]
        kd = (kr * kdec_ref[:, h:h + 1]).astype(BF16)
        s_scr[h] = s * cdec[h] + _dot_tn(kd, vb)
        on = o * lax.rsqrt(jnp.mean(o * o, axis=-1, keepdims=True) + EPS) * retg_ref[:, sl]
        ro_ref[:, sl] = (on * _silu(gr[:, sl]))[:c_in].astype(ro_ref.dtype)

    @pl.when(c == pl.num_programs(1) - 1)
    def _():
        so_ref[0] = s_scr[...]


def _even_mixer(z, s0, ln_g, ln_b, w_s, b_s, ret_g, *, n_batch, seq, pos0, emit_vn, out_dtype):
    rows, n = z.shape
    w = n // 7
    c_in = min(seq, CHUNK)
    n_chunks = seq // c_in
    c_len = CHUNK if seq % CHUNK == 0 else seq
    dmask, qdec, kdec, cdec, ang = _retention_tables(c_len, pos0, n_chunks * CHUNK)
    cos = jnp.asarray(np.cos(ang).astype(np.float32))
    sin = jnp.asarray(np.sin(ang).astype(np.float32))

    def zspec(col):
        return pl.BlockSpec((c_in, w), lambda b, c, col=col: (b * n_chunks + c, col))

    def const(shape):
        return pl.BlockSpec(shape, lambda b, c: (0,) * len(shape))

    row_spec = pl.BlockSpec((c_in, w), lambda b, c: (b * n_chunks + c, 0))
    state_spec = pl.BlockSpec((1,) + s0.shape[1:], lambda b, c: (b, 0, 0, 0))
    out_specs = [row_spec, row_spec, state_spec]
    out_shape = [jax.ShapeDtypeStruct((rows, w), out_dtype),
                 jax.ShapeDtypeStruct((rows, w), out_dtype),
                 jax.ShapeDtypeStruct(s0.shape, F32)]
    if emit_vn:
        out_specs.append(row_spec)
        out_shape.append(jax.ShapeDtypeStruct((rows, w), F32))
    return pl.pallas_call(
        functools.partial(_even_mixer_kernel, c_in=c_in, cdec=cdec, emit_vn=emit_vn),
        grid=(n_batch, n_chunks),
        in_specs=[zspec(i) for i in range(7)] + [
            const((1, w)), const((1, w)), const(w_s.shape), const((CHUNK, A_GROUPS)), const((1, w)),
            pl.BlockSpec((CHUNK, LANES), lambda b, c: (c, 0)),
            pl.BlockSpec((CHUNK, LANES), lambda b, c: (c, 0)),
            const(dmask.shape), const(qdec.shape), const(kdec.shape),
            state_spec,
        ],
        out_specs=out_specs,
        out_shape=out_shape,
        scratch_shapes=[pltpu.VMEM(s0.shape[1:], F32)],
        compiler_params=_cparams("arbitrary", "arbitrary"),
        name="even_mixer",
    )(z, z, z, z, z, z, z, ln_g.reshape(1, w), ln_b.reshape(1, w), w_s, b_s.T, ret_g.reshape(1, w),
      cos, sin, jnp.asarray(dmask), jnp.asarray(qdec), jnp.asarray(kdec), s0)


POOL_PREV = 16
POOL_LEAD = SUBLANES


def _pool_kernel(cin_ref, cg_ref, prev_ref, cnt_ref, wp_ref, ps_ref, o_ref, e0, e1, e2, e3, e4, *, tq, t_pad):
    t = pl.program_id(1)
    base = POOL_LEAD + POOL_PREV
    total = base + t_pad
    w = cin_ref.shape[1]
    gw = w // len(POOL_WINDOWS)

    @pl.when(t == 0)
    def _():
        for e in (e0, e1, e2, e3, e4):
            e[0:POOL_LEAD, :] = jnp.zeros((POOL_LEAD, w), F32)
        e0[POOL_LEAD:base, :] = prev_ref[0]
        if t_pad > tq:
            e0[base + tq:total, :] = jnp.zeros((t_pad - tq, w), F32)

    e0[base:base + tq, :] = cin_ref[...]
    levels = (e0, e1, e2, e3, e4)
    for lvl in range(1, 5):
        shift = 1 << (lvl - 1)
        c0 = (lvl - 1) * gw
        src, dst = levels[lvl - 1], levels[lvl]
        dst[POOL_LEAD:total, c0:] = (src[POOL_LEAD:total, c0:]
                                     + src[POOL_LEAD - shift:total - shift, c0:])
    cur = e0[base:total, :]
    cg = cg_ref[...]
    for gi in range(len(POOL_WINDOWS)):
        sl = slice(gi * gw, (gi + 1) * gw)
        win = levels[gi + 1][base:total, sl]
        pooled = win / cnt_ref[:, gi:gi + 1] - cur[:, sl]
        mixed = _dot(pooled.astype(BF16), wp_ref[gi]) * ps_ref[:, sl]
        o_ref[:, sl] = (mixed[:tq] * _silu(cg[:, sl])).astype(o_ref.dtype)
    e0[POOL_LEAD:base, :] = e0[POOL_LEAD + t_pad:base + t_pad, :]


def _pool_branch(z_main, prev, cnt, wp_bf, pool_scale, *, n_batch, seq, tq, out_dtype):
    rows = z_main.shape[0]
    w = wp_bf.shape[0] * wp_bf.shape[1]
    t_pad = max(tq, POOL_PREV)
    n_t = seq // tq
    ext = pltpu.VMEM((POOL_LEAD + POOL_PREV + t_pad, w), F32)
    return pl.pallas_call(
        functools.partial(_pool_kernel, tq=tq, t_pad=t_pad),
        grid=(n_batch, n_t),
        in_specs=[
            pl.BlockSpec((tq, w), lambda b, t: (b * n_t + t, 0)),
            pl.BlockSpec((tq, w), lambda b, t: (b * n_t + t, 1)),
            pl.BlockSpec((1, POOL_PREV, w), lambda b, t: (b, 0, 0)),
            pl.BlockSpec((t_pad, len(POOL_WINDOWS)), lambda b, t: (t, 0)),
            pl.BlockSpec(wp_bf.shape, lambda b, t: (0, 0, 0)),
            pl.BlockSpec((1, w), lambda b, t: (0, 0)),
        ],
        out_specs=pl.BlockSpec((tq, w), lambda b, t: (b * n_t + t, 0)),
        out_shape=jax.ShapeDtypeStruct((rows, w), out_dtype),
        scratch_shapes=[ext] * 5,
        compiler_params=_cparams("arbitrary", "arbitrary"),
        name="pool_branch",
    )(z_main, z_main, prev, cnt, wp_bf, pool_scale.reshape(1, w))


def _pool_counts(pos0, n_rows):
    pos = pos0 + np.arange(n_rows)
    return jnp.asarray(np.stack([np.minimum(pos + 1, wd) for wd in POOL_WINDOWS], axis=1).astype(np.float32))


def _split3(x):
    a = x.astype(BF16)
    r1 = x - a.astype(F32)
    b = r1.astype(BF16)
    c = (r1 - b.astype(F32)).astype(BF16)
    return a, b, c


def _fcum_kernel(lf_ref, fc_ref, fr_ref, *, blk):
    n_blk = lf_ref.shape[0] // blk
    tri = (lax.broadcasted_iota(jnp.int32, (blk, blk), 0)
           >= lax.broadcasted_iota(jnp.int32, (blk, blk), 1)).astype(BF16)
    carry = jnp.zeros((1, LANES), F32)
    for i in range(n_blk):
        a, b, c = _split3(lf_ref[i * blk:(i + 1) * blk, :])
        cs = (_dot(tri, a) + _dot(tri, b)) + _dot(tri, c) + carry
        carry = cs[blk - 1:blk, :]
        fc_ref[i * blk:(i + 1) * blk, :] = cs
        fr_ref[0, i] = cs.T[:D_HEADS, :]


def _fcum(lf, *, n_batch, seq, blk):
    return pl.pallas_call(
        functools.partial(_fcum_kernel, blk=blk),
        grid=(n_batch,),
        in_specs=[pl.BlockSpec((seq, LANES), lambda b: (b, 0))],
        out_specs=[pl.BlockSpec((seq, LANES), lambda b: (b, 0)),
                   pl.BlockSpec((1, seq // blk, D_HEADS, blk), lambda b: (b, 0, 0, 0))],
        out_shape=[jax.ShapeDtypeStruct((n_batch * seq, LANES), F32),
                   jax.ShapeDtypeStruct((n_batch, seq // blk, D_HEADS, blk), F32)],
        compiler_params=_cparams("arbitrary"),
        name="forget_cumsum",
    )(lf)


def _attn_prompt_kernel(q_ref, dg_ref, k_ref, v_ref, fc_ref, fr_ref, o_ref,
                        kb_scr, vb_scr, m_scr, l_scr, acc_scr, *, tq):
    i = pl.program_id(1)
    dh = LANES
    scale = dh ** -0.5

    @pl.when(i == 0)
    def _():
        kb_scr[...] = k_ref[0].astype(BF16)
        vb_scr[...] = v_ref[0].astype(BF16)

    tril = (lax.broadcasted_iota(jnp.int32, (tq, tq), 0)
            >= lax.broadcasted_iota(jnp.int32, (tq, tq), 1))
    for h in range(D_HEADS):
        sl = slice(h * dh, (h + 1) * dh)
        qb = q_ref[:, sl].astype(BF16)
        fq = fc_ref[:, h:h + 1]
        m_scr[...] = jnp.full(m_scr.shape, NEG_BIG, F32)
        l_scr[...] = jnp.zeros(l_scr.shape, F32)
        acc_scr[...] = jnp.zeros(acc_scr.shape, F32)

        def block(kb, masked):
            r0 = pl.multiple_of(kb * tq, tq)
            s = (_dot_nt(qb, kb_scr[pl.ds(r0, tq), sl]) * scale + fq) - fr_ref[0, kb, h:h + 1, :]
            if masked:
                s = jnp.where(tril, s, NEG_BIG)
            m_old = m_scr[...]
            m_new = jnp.maximum(m_old, jnp.max(s, axis=-1, keepdims=True))
            alpha = jnp.exp(m_old - m_new)
            p = jnp.exp(s - m_new)
            l_scr[...] = alpha * l_scr[...] + jnp.sum(p, axis=-1, keepdims=True)
            acc_scr[...] = alpha * acc_scr[...] + _dot(p.astype(BF16), vb_scr[pl.ds(r0, tq), sl])
            m_scr[...] = m_new

        def body(kb, carry):
            block(kb, False)
            return carry

        lax.fori_loop(0, i, body, 0)
        block(i, True)
        o_ref[:, sl] = ((acc_scr[...] / l_scr[...]) * _silu(dg_ref[:, sl])).astype(o_ref.dtype)


def _attn_prompt(z_main, k_all, v_all, fcol, frow, *, layer_slot, n_batch, seq, tq, out_dtype):
    rows = z_main.shape[0]
    w = k_all.shape[2]
    n_q = seq // tq
    return pl.pallas_call(
        functools.partial(_attn_prompt_kernel, tq=tq),
        grid=(n_batch, n_q),
        in_specs=[
            pl.BlockSpec((tq, w), lambda b, i: (b * n_q + i, 2)),
            pl.BlockSpec((tq, w), lambda b, i: (b * n_q + i, 3)),
            pl.BlockSpec((1, seq, w), lambda b, i: (layer_slot, b, 0)),
            pl.BlockSpec((1, seq, w), lambda b, i: (layer_slot, b, 0)),
            pl.BlockSpec((tq, LANES), lambda b, i: (b * n_q + i, 0)),
            pl.BlockSpec((1, n_q, D_HEADS, tq), lambda b, i: (b, 0, 0, 0)),
        ],
        out_specs=pl.BlockSpec((tq, w), lambda b, i: (b * n_q + i, 0)),
        out_shape=jax.ShapeDtypeStruct((rows, w), out_dtype),
        scratch_shapes=[pltpu.VMEM((seq, w), BF16), pltpu.VMEM((seq, w), BF16),
                        pltpu.VMEM((tq, 1), F32), pltpu.VMEM((tq, 1), F32), pltpu.VMEM((tq, LANES), F32)],
        compiler_params=_cparams("arbitrary", "arbitrary"),
        name="attn_prompt",
    )(z_main, z_main, k_all, v_all, fcol, frow)


def _attn_sample_tables(n_new):
    pg = np.arange(LANES)
    same_head = (pg[:, None] % D_HEADS) == (pg[None, :] % D_HEADS)
    later = (pg[:, None] // D_HEADS) > (pg[None, :] // D_HEADS)
    c_later = (same_head & later).astype(np.float32)
    c_same = same_head.astype(np.float32)
    n_pg = PAGE_SIZE * D_HEADS
    r = np.arange(D_HEADS * n_new)
    col = np.arange(n_pg)
    past_ok = (col[None, :] % D_HEADS) == (r[:, None] // n_new)
    past_bias = np.where(past_ok, 0.0, NEG_BIG).astype(np.float32)
    cn = np.arange(n_new * D_HEADS)
    new_ok = ((cn[None, :] % D_HEADS) == (r[:, None] // n_new)) & ((cn[None, :] // D_HEADS) <= (r[:, None] % n_new))
    new_bias = np.where(new_ok, 0.0, NEG_BIG).astype(np.float32)
    m_cols = ((cn[:, None] % D_HEADS == cn[None, :] % D_HEADS)
              & (cn[:, None] // D_HEADS <= cn[None, :] // D_HEADS)).astype(np.float32)
    m_rows = ((cn[None, :] % D_HEADS == r[:, None] // n_new)
              & (cn[None, :] // D_HEADS <= r[:, None] % n_new)).astype(np.float32)
    return c_later, c_same, past_bias, new_bias, m_cols, m_rows


def _attn_sample_kernel(pt_ref, q_ref, kp_ref, vp_ref, lfp_ref, kn_ref, vn_ref, lfr_ref, lfc_ref, dg_ref,
                        cl_ref, cs_ref, pb_ref, nb_ref, mc_ref, mr_ref, o_ref,
                        m_scr, l_scr, acc_scr, run_scr, *, n_pages):
    del pt_ref
    p = pl.program_id(1)
    scale = LANES ** -0.5
    hi = lax.Precision.HIGHEST

    @pl.when(p == 0)
    def _():
        m_scr[...] = jnp.full(m_scr.shape, NEG_BIG, F32)
        l_scr[...] = jnp.zeros(l_scr.shape, F32)
        acc_scr[...] = jnp.zeros(acc_scr.shape, F32)
        run_scr[...] = jnp.zeros(run_scr.shape, F32)

    qb = q_ref[0].astype(BF16)
    rowc = jnp.sum(mr_ref[...] * lfr_ref[0], axis=-1, keepdims=True)

    lfp = lfp_ref[0, 0]
    within = jnp.dot(lfp, cl_ref[...], precision=hi, preferred_element_type=F32)
    rowtot = jnp.dot(lfp, cs_ref[...], precision=hi, preferred_element_type=F32)
    run = run_scr[...]
    g_rows = [None] * SUBLANES
    for a in range(SUBLANES - 1, -1, -1):
        g_rows[a] = within[a:a + 1, :] + run
        run = run + rowtot[a:a + 1, :]
    run_scr[...] = run
    g_past = jnp.concatenate(g_rows, axis=1)
    s = (_dot_nt(qb, kp_ref[0, 0].astype(BF16)) * scale + rowc) + g_past + pb_ref[...]
    m_old = m_scr[...]
    m_new = jnp.maximum(m_old, jnp.max(s, axis=-1, keepdims=True))
    alpha = jnp.exp(m_old - m_new)
    pr = jnp.exp(s - m_new)
    l_scr[...] = alpha * l_scr[...] + jnp.sum(pr, axis=-1, keepdims=True)
    acc_scr[...] = alpha * acc_scr[...] + _dot(pr.astype(BF16), vp_ref[0, 0].astype(BF16))
    m_scr[...] = m_new

    @pl.when(p == n_pages - 1)
    def _():
        c_new = jnp.sum(mc_ref[...] * lfc_ref[0], axis=0, keepdims=True)
        s = (_dot_nt(qb, kn_ref[0].astype(BF16)) * scale + rowc) - c_new + nb_ref[...]
        m_old = m_scr[...]
        m_new = jnp.maximum(m_old, jnp.max(s, axis=-1, keepdims=True))
        alpha = jnp.exp(m_old - m_new)
        pr = jnp.exp(s - m_new)
        l_fin = alpha * l_scr[...] + jnp.sum(pr, axis=-1, keepdims=True)
        acc = alpha * acc_scr[...] + _dot(pr.astype(BF16), vn_ref[0].astype(BF16))
        o_ref[0] = (acc / l_fin) * _silu(dg_ref[0])


def _attn_sample(q_hq, kn, vn, lf_row, lf_col, dg_hq, cache_k, cache_v, cache_lf, page_table, *, layer_slot):
    bd, r, dh = q_hq.shape
    n_pages = page_table.shape[1]
    n_new = r // D_HEADS
    n_slots, n_phys = cache_k.shape[:2]
    pg_rows = PAGE_SIZE * D_HEADS
    ck = cache_k.reshape(n_slots, n_phys, pg_rows, dh)
    cv = cache_v.reshape(n_slots, n_phys, pg_rows, dh)
    clf = cache_lf.reshape(n_slots, n_phys, SUBLANES, LANES)
    tabs = [jnp.asarray(t) for t in _attn_sample_tables(n_new)]

    def page(b, p, pt):
        return (layer_slot, pt[b, n_pages - 1 - p], 0, 0)

    def per_b(shape):
        return pl.BlockSpec((1,) + shape, lambda b, p, pt: (b, 0, 0))

    def const(t):
        return pl.BlockSpec(t.shape, lambda b, p, pt: (0, 0))

    grid_spec = pltpu.PrefetchScalarGridSpec(
        num_scalar_prefetch=1,
        grid=(bd, n_pages),
        in_specs=[
            per_b((r, dh)),
            pl.BlockSpec((1, 1, pg_rows, dh), page),
            pl.BlockSpec((1, 1, pg_rows, dh), page),
            pl.BlockSpec((1, 1, SUBLANES, LANES), page),
            per_b((r, dh)), per_b((r, dh)), per_b((1, r)), per_b((r, 1)), per_b((r, dh)),
        ] + [const(t) for t in tabs],
        out_specs=per_b((r, dh)),
        scratch_shapes=[pltpu.VMEM((r, 1), F32), pltpu.VMEM((r, 1), F32), pltpu.VMEM((r, dh), F32),
                        pltpu.VMEM((1, LANES), F32)],
    )
    return pl.pallas_call(
        functools.partial(_attn_sample_kernel, n_pages=n_pages),
        grid_spec=grid_spec,
        out_shape=jax.ShapeDtypeStruct((bd, r, dh), F32),
        compiler_params=_cparams("arbitrary", "arbitrary"),
        name="attn_sample",
    )(page_table, q_hq, ck, cv, clf, kn, vn, lf_row, lf_col, dg_hq, *tabs)


TM_PROMPT = 1024
TM_OUT = 512
TQ_POOL = 256
TQ_ATTN = 256


def kernel(x_prompt, x_sample, c_prompt, c_sample, state_ret, state_pool, cache_k, cache_v, cache_logf,
           page_table, g_pre, g_post, w_ada, b_ada, w_in_even, w_out_even, ln_a_g, ln_a_b, w_s, b_s,
           ret_g, w_in_odd, b_f, w_out_odd, w_pool, pool_scale):
    bp, seq, d = x_prompt.shape
    bd, n_new, _ = x_sample.shape
    depth = g_pre.shape[0]
    n_odd = w_in_odd.shape[0]
    w = d // 2
    n_past = page_table.shape[1] * PAGE_SIZE
    rows_s = bd * n_new

    c_all = jnp.concatenate([c_prompt, c_sample], axis=0)
    c_all = jnp.pad(c_all, ((0, -c_all.shape[0] % (2 * SUBLANES)), (0, 0)))
    mod = _ada_mod(c_all, w_ada, b_ada)

    def mods(l):
        mp = mod[l, :bp].reshape(bp, 1, 3 * d)
        ms = jnp.repeat(mod[l, bp:bp + bd], n_new, axis=0).reshape(1, rows_s, 3 * d)
        return [(m[..., :d], m[..., d:2 * d], m[..., 2 * d:]) for m in (mp, ms)]

    xp = x_prompt.reshape(bp * seq, d)
    xs = x_sample.reshape(rows_s, d)
    ret_p, ret_s, gv_s, pool_p, pool_s, lfp_l, lfs_l = [], [], [], [], [], [], []
    kv_p = kv_s = None
    zeros_state = jnp.zeros((bp,) + state_ret.shape[2:], F32)
    for l in range(depth):
        j = l // 2
        (shift_p, scale_p, gate_p), (shift_s, scale_s, gate_s) = mods(l)
        gpre = g_pre[l].reshape(1, d)
        gpost = g_post[l].reshape(1, d)
        if l % 2 == 0:
            w_in = w_in_even[j].astype(BF16)
            w_out = w_out_even[j].astype(BF16)
            zp = _inproj_even(xp, scale_p, shift_p, gpre, w_in, per_row=False, rows_per_mod=seq, tm=TM_PROMPT)
            zs = _inproj_even(xs, scale_s, shift_s, gpre, w_in, per_row=True, rows_per_mod=1, tm=rows_s)
            mix = functools.partial(_even_mixer, ln_g=ln_a_g[j], ln_b=ln_a_b[j], w_s=w_s[j], b_s=b_s[j],
                                    ret_g=ret_g[j])
            ap, rp, sp = mix(zp, zeros_state, n_batch=bp, seq=seq, pos0=0, emit_vn=False, out_dtype=BF16)
            as_, rs, ss, vn_s = mix(zs, state_ret[j], n_batch=bd, seq=n_new, pos0=n_past, emit_vn=True,
                                    out_dtype=F32)
            ret_p.append(sp)
            ret_s.append(ss)
            gv_s.append(vn_s.reshape(bd, n_new, w))
        else:
            w_in = w_in_odd[j, :, :6 * w].astype(BF16)
            wf = jnp.pad(w_in_odd[j, :, 6 * w:], ((0, 0), (0, LANES - D_HEADS))).astype(BF16)
            bf = jnp.pad(b_f[j], (0, LANES - D_HEADS)).reshape(1, LANES)
            w_out = w_out_odd[j].astype(BF16)
            wp = w_pool[j].astype(BF16)
            zp, kp_all, vp_all, lfp = _inproj_odd(xp, scale_p, shift_p, gpre, w_in, wf, bf, kv_p,
                                                  layer_slot=j, n_slots=n_odd, per_row=False,
                                                  rows_per_mod=seq, tm=TM_PROMPT)
            kv_p = (kp_all, vp_all)
            zs, ks_all, vs_all, lfs = _inproj_odd(xs, scale_s, shift_s, gpre, w_in, wf, bf, kv_s,
                                                  layer_slot=j, n_slots=n_odd, per_row=True,
                                                  rows_per_mod=1, tm=rows_s)
            kv_s = (ks_all, vs_all)
            ap = _pool_branch(zp, jnp.zeros((bp, POOL_PREV, w), F32), _pool_counts(0, seq), wp, pool_scale[j],
                              n_batch=bp, seq=seq, tq=TQ_POOL, out_dtype=BF16)
            fcol, frow = _fcum(lfp, n_batch=bp, seq=seq, blk=TQ_ATTN)
            rp = _attn_prompt(zp, kp_all, vp_all, fcol, frow, layer_slot=j, n_batch=bp, seq=seq, tq=TQ_ATTN,
                              out_dtype=BF16)
            pool_p.append(zp.reshape(bp, seq, 4 * w)[:, seq - POOL_BUF:, :w])
            lfp_l.append(lfp[:, :D_HEADS].reshape(bp, seq, D_HEADS))
            prev = jnp.pad(state_pool[j], ((0, 0), (POOL_PREV - POOL_BUF, 0), (0, 0)))
            as_ = _pool_branch(zs, prev, _pool_counts(n_past, POOL_PREV), wp, pool_scale[j],
                               n_batch=bd, seq=n_new, tq=n_new, out_dtype=F32)
            zs4 = zs.reshape(bd, n_new, 4, D_HEADS, LANES)
            to_hq = lambda a: a.transpose(0, 2, 1, 3).reshape(bd, D_HEADS * n_new, LANES)
            lf_new = lfs[:, :D_HEADS].reshape(bd, n_new * D_HEADS)
            o_hq = _attn_sample(to_hq(zs4[:, :, 2]), ks_all[j].reshape(bd, n_new * D_HEADS, LANES),
                                vs_all[j].reshape(bd, n_new * D_HEADS, LANES),
                                lf_new.reshape(bd, 1, -1), lf_new.reshape(bd, -1, 1), to_hq(zs4[:, :, 3]),
                                cache_k, cache_v, cache_logf, page_table, layer_slot=j)
            rs = o_hq.reshape(bd, D_HEADS, n_new, LANES).transpose(0, 2, 1, 3).reshape(rows_s, w)
            c_in_s = zs[:, :w].reshape(bd, n_new, w)
            pool_s.append(jnp.concatenate([state_pool[j], c_in_s], axis=1)[:, -POOL_BUF:])
            lfs_l.append(lfs[:, :D_HEADS].reshape(bd, n_new, D_HEADS))
        xp = _outproj(ap, rp, xp, gate_p, gpost, w_out, per_row=False, rows_per_mod=seq, tm=TM_OUT)
        xs = _outproj(as_, rs, xs, gate_s, gpost, w_out, per_row=True, rows_per_mod=1, tm=rows_s)

    dh = LANES
    return (xp.reshape(bp, seq, d), xs.reshape(bd, n_new, d),
            jnp.stack(ret_p), jnp.stack(ret_s), jnp.stack(gv_s), jnp.stack(pool_p), jnp.stack(pool_s),
            kv_p[0].reshape(n_odd, bp, seq, D_HEADS, dh), kv_p[1].reshape(n_odd, bp, seq, D_HEADS, dh),
            jnp.stack(lfp_l),
            kv_s[0].reshape(n_odd, bd, n_new, D_HEADS, dh), kv_s[1].reshape(n_odd, bd, n_new, D_HEADS, dh),
            jnp.stack(lfs_l))
```

```python
import functools

import numpy as np
import jax
import jax.numpy as jnp
from jax import lax
from jax.experimental import pallas as pl
from jax.experimental.pallas import tpu as pltpu

F32 = jnp.float32
BF16 = jnp.bfloat16

EPS = 1e-6
ROPE_BASE = 10000.0
CHUNK = 128
A_GROUPS = 8
B_HEADS = 4
POOL_WINDOWS = (2, 4, 8, 16)
POOL_BUF = 15
D_HEADS = 8
PAGE_SIZE = 128
LANES = 128
SUBLANES = 8
NEG_BIG = -1e30
VMEM_LIMIT = 56 * 1024 * 1024


def _cparams(*sem):
    return pltpu.CompilerParams(dimension_semantics=sem, vmem_limit_bytes=VMEM_LIMIT)


def _silu(x):
    return x * jax.nn.sigmoid(x)


def _log_sigmoid(x):
    return jnp.minimum(x, 0.0) - jnp.log1p(jnp.exp(-jnp.abs(x)))


def _dot(a, b):
    return jnp.dot(a, b, preferred_element_type=F32)


def _dot_nt(a, b):
    return lax.dot_general(a, b, (((1,), (1,)), ((), ())), preferred_element_type=F32)


def _dot_tn(a, b):
    return lax.dot_general(a, b, (((0,), (0,)), ((), ())), preferred_element_type=F32)


def _ada_kernel(c_ref, w_ref, b_ref, o_ref):
    a = _silu(c_ref[...]).astype(BF16)
    o_ref[0] = _dot(a, w_ref[0].astype(BF16)) + b_ref[0]


def _ada_mod(c_all, w_ada, b_ada, tn=1024):
    depth, d, n = w_ada.shape
    r = c_all.shape[0]
    return pl.pallas_call(
        _ada_kernel,
        grid=(depth, n // tn),
        in_specs=[
            pl.BlockSpec((r, d), lambda l, j: (0, 0)),
            pl.BlockSpec((1, d, tn), lambda l, j: (l, 0, j)),
            pl.BlockSpec((1, 1, tn), lambda l, j: (l, 0, j)),
        ],
        out_specs=pl.BlockSpec((1, r, tn), lambda l, j: (l, 0, j)),
        out_shape=jax.ShapeDtypeStruct((depth, r, n), F32),
        compiler_params=_cparams("arbitrary", "arbitrary"),
        name="ada_mod",
    )(c_all, w_ada, b_ada.reshape(depth, 1, n))


def _modnorm_to(h_scr, x_ref, sc_ref, sh_ref, g_ref):
    x = x_ref[...]
    y = x * lax.rsqrt(jnp.mean(x * x, axis=-1, keepdims=True) + EPS) * g_ref[...]
    h_scr[...] = (y * (1.0 + sc_ref[0]) + sh_ref[0]).astype(BF16)


def _inproj_even_kernel(x_ref, sc_ref, sh_ref, g_ref, w_ref, z_ref, h_scr):
    @pl.when(pl.program_id(1) == 0)
    def _():
        _modnorm_to(h_scr, x_ref, sc_ref, sh_ref, g_ref)

    z_ref[...] = _dot(h_scr[...], w_ref[0])


def _inproj_odd_kernel(x_ref, sc_ref, sh_ref, g_ref, w_ref, wf_ref, bf_ref, *rest, tiles_per_group, kv_slot):
    z_ref, k_ref, v_ref, lf_ref, h_scr = rest[-5:]
    j = pl.program_id(1)

    def put(kv_ref, z):
        for s in range(kv_ref.shape[0]):
            kv_ref[s] = z if s == kv_slot else jnp.zeros_like(z)

    @pl.when(j == 0)
    def _():
        _modnorm_to(h_scr, x_ref, sc_ref, sh_ref, g_ref)
        lf_ref[...] = _log_sigmoid(_dot(h_scr[...], wf_ref[...]) + bf_ref[...])

    z = _dot(h_scr[...], w_ref[0])
    group = j // tiles_per_group

    @pl.when(jnp.logical_or(group < 3, group == 5))
    def _():
        z_ref[...] = z

    @pl.when(group == 3)
    def _():
        put(k_ref, z)

    @pl.when(group == 4)
    def _():
        put(v_ref, z)


def _mod_specs(per_row, rows_per_mod, tm, d):
    if per_row:
        return pl.BlockSpec((1, tm, d), lambda i, j: (0, i, 0))
    return pl.BlockSpec((1, 1, d), lambda i, j: ((i * tm) // rows_per_mod, 0, 0))


def _inproj_even(x2d, scale, shift, g, w_bf, *, layer_slot, per_row, rows_per_mod, tm, tn=512):
    rows, d = x2d.shape
    n = w_bf.shape[2]
    return pl.pallas_call(
        _inproj_even_kernel,
        grid=(rows // tm, n // tn),
        in_specs=[
            pl.BlockSpec((tm, d), lambda i, j: (i, 0)),
            _mod_specs(per_row, rows_per_mod, tm, d),
            _mod_specs(per_row, rows_per_mod, tm, d),
            pl.BlockSpec((1, d), lambda i, j: (0, 0)),
            pl.BlockSpec((1, d, tn), lambda i, j: (layer_slot, 0, j)),
        ],
        out_specs=pl.BlockSpec((tm, tn), lambda i, j: (i, j)),
        out_shape=jax.ShapeDtypeStruct((rows, n), F32),
        scratch_shapes=[pltpu.VMEM((tm, d), BF16)],
        compiler_params=_cparams("arbitrary", "arbitrary"),
        name="inproj_even",
    )(x2d, scale, shift, g, w_bf)


def _inproj_odd(x2d, scale, shift, g, w_bf, wf_bf, bf, kv_prev, *, layer_slot, n_slots,
                per_row, rows_per_mod, tm, tn=512):
    rows, d = x2d.shape
    w = (w_bf.shape[2] - D_HEADS) // 6
    tpg = w // tn
    n_tiles = 6 * tpg

    creates_kv = kv_prev is None
    kv_block = (n_slots if creates_kv else 1, tm, tn)
    kv_first = 0 if creates_kv else layer_slot

    def zmap(i, j):
        return (i, j - jnp.clip(j - (3 * tpg - 1), 0, 2 * tpg))

    def kmap(i, j):
        return (kv_first, i, jnp.clip(j - 3 * tpg, 0, tpg - 1))

    def vmap(i, j):
        return (kv_first, i, jnp.clip(j - 4 * tpg, 0, tpg - 1))

    in_specs = [
        pl.BlockSpec((tm, d), lambda i, j: (i, 0)),
        _mod_specs(per_row, rows_per_mod, tm, d),
        _mod_specs(per_row, rows_per_mod, tm, d),
        pl.BlockSpec((1, d), lambda i, j: (0, 0)),
        pl.BlockSpec((1, d, tn), lambda i, j: (layer_slot, 0, j)),
        pl.BlockSpec((d, LANES), lambda i, j: (0, 0)),
        pl.BlockSpec((1, LANES), lambda i, j: (0, 0)),
    ]
    args = [x2d, scale, shift, g, w_bf, wf_bf, bf]
    aliases = {}
    if kv_prev is not None:
        in_specs += [pl.BlockSpec(memory_space=pl.ANY)] * 2
        args += list(kv_prev)
        aliases = {7: 1, 8: 2}
    kv_shape = jax.ShapeDtypeStruct((n_slots, rows, w), F32)
    return pl.pallas_call(
        functools.partial(_inproj_odd_kernel, tiles_per_group=tpg, kv_slot=layer_slot if creates_kv else 0),
        grid=(rows // tm, n_tiles),
        in_specs=in_specs,
        out_specs=[
            pl.BlockSpec((tm, tn), zmap),
            pl.BlockSpec(kv_block, kmap),
            pl.BlockSpec(kv_block, vmap),
            pl.BlockSpec((tm, LANES), lambda i, j: (i, 0)),
        ],
        out_shape=[
            jax.ShapeDtypeStruct((rows, 4 * w), F32),
            kv_shape,
            kv_shape,
            jax.ShapeDtypeStruct((rows, LANES), F32),
        ],
        scratch_shapes=[pltpu.VMEM((tm, d), BF16)],
        input_output_aliases=aliases,
        compiler_params=_cparams("arbitrary", "arbitrary"),
        name="inproj_odd",
    )(*args)


def _outproj_kernel(a_ref, b_ref, x_ref, gate_ref, g_ref, w1_ref, w2_ref, o_ref):
    y = _dot(a_ref[...].astype(BF16), w1_ref[0]) + _dot(b_ref[...].astype(BF16), w2_ref[0])
    yn = y * lax.rsqrt(jnp.mean(y * y, axis=-1, keepdims=True) + EPS) * g_ref[...]
    o_ref[...] = x_ref[...] + gate_ref[0] * yn


def _outproj(a, b, x2d, gate, g, w_bf, *, layer_slot, per_row, rows_per_mod, tm):
    rows, d = x2d.shape
    w = a.shape[1]
    if per_row:
        gate_spec = pl.BlockSpec((1, tm, d), lambda i: (0, i, 0))
    else:
        gate_spec = pl.BlockSpec((1, 1, d), lambda i: ((i * tm) // rows_per_mod, 0, 0))
    return pl.pallas_call(
        _outproj_kernel,
        grid=(rows // tm,),
        in_specs=[
            pl.BlockSpec((tm, w), lambda i: (i, 0)),
            pl.BlockSpec((tm, w), lambda i: (i, 0)),
            pl.BlockSpec((tm, d), lambda i: (i, 0)),
            gate_spec,
            pl.BlockSpec((1, d), lambda i: (0, 0)),
            pl.BlockSpec((1, w, d), lambda i: (layer_slot, 0, 0)),
            pl.BlockSpec((1, w, d), lambda i: (layer_slot, 1, 0)),
        ],
        out_specs=pl.BlockSpec((tm, d), lambda i: (i, 0)),
        out_shape=jax.ShapeDtypeStruct((rows, d), F32),
        compiler_params=_cparams("arbitrary"),
        name="outproj",
    )(a, b, x2d, gate, g, w_bf, w_bf)


def _retention_tables(c_len, pos0, n_rows):
    lg = np.log(1.0 - 2.0 ** (-5.0 - np.arange(B_HEADS, dtype=np.float64)))
    t = np.arange(CHUNK, dtype=np.float64)
    diff = t[:, None] - t[None, :]
    dmask = np.where(diff >= 0, np.exp(lg[:, None, None] * np.maximum(diff, 0.0)), 0.0).astype(np.float32)
    qdec = np.exp(lg[None, :] * (t + 1.0)[:, None]).astype(np.float32)
    kdec = np.exp(lg[None, :] * np.maximum(c_len - 1.0 - t, 0.0)[:, None]).astype(np.float32)
    cdec = tuple(float(v) for v in np.exp(lg * c_len).astype(np.float32))
    half = LANES
    inv = ROPE_BASE ** (-np.arange(half, dtype=np.float64) / half)
    pos = (pos0 + np.arange(n_rows)).astype(np.float64)
    ang = pos[:, None] * inv[None, :]
    return dmask, qdec, kdec, cdec, ang


def _even_mixer_kernel(u_ref, v_ref, ga_ref, q_ref, k_ref, vv_ref, gr_ref,
                       lng_ref, lnb_ref, ws_ref, bst_ref, retg_ref, cos_ref, sin_ref,
                       dmask_ref, qdec_ref, kdec_ref, s0_ref, *rest, c_in, cdec, emit_vn):
    if emit_vn:
        ao_ref, ro_ref, so_ref, vn_ref, s_scr = rest
    else:
        ao_ref, ro_ref, so_ref, s_scr = rest
    c = pl.program_id(1)

    @pl.when(c == 0)
    def _():
        s_scr[...] = s0_ref[0]

    def ld(ref):
        x = ref[...]
        if c_in == CHUNK:
            return x
        return jnp.concatenate([x, jnp.zeros((CHUNK - c_in, x.shape[1]), F32)], axis=0)

    v = ld(v_ref)
    xc = v - jnp.mean(v, axis=-1, keepdims=True)
    vn = xc * lax.rsqrt(jnp.mean(xc * xc, axis=-1, keepdims=True) + EPS) * lng_ref[...] + lnb_ref[...]
    if emit_vn:
        vn_ref[...] = vn[:c_in]
    vnb = vn.astype(BF16)
    u = ld(u_ref)
    ga = ld(ga_ref)
    causal = (lax.broadcasted_iota(jnp.int32, (CHUNK, CHUNK), 0)
              >= lax.broadcasted_iota(jnp.int32, (CHUNK, CHUNK), 1))
    gw = u.shape[1] // A_GROUPS
    for g in range(A_GROUPS):
        sl = slice(g * gw, (g + 1) * gw)
        wm = jnp.where(causal, ws_ref[g], 0.0).astype(BF16)
        mixed = _dot(wm, vnb[:, sl]) + bst_ref[:, g:g + 1]
        ao_ref[:, sl] = (u[:, sl] * mixed * _silu(ga[:, sl]))[:c_in].astype(ao_ref.dtype)

    cos = cos_ref[...]
    sin = sin_ref[...]
    q = ld(q_ref)
    k = ld(k_ref)
    vv = ld(vv_ref)
    gr = ld(gr_ref)
    dk = q.shape[1] // B_HEADS
    half = dk // 2
    for h in range(B_HEADS):
        sl = slice(h * dk, (h + 1) * dk)
        q1, q2 = q[:, h * dk:h * dk + half], q[:, h * dk + half:(h + 1) * dk]
        k1, k2 = k[:, h * dk:h * dk + half], k[:, h * dk + half:(h + 1) * dk]
        qr = jnp.concatenate([q1 * cos - q2 * sin, q1 * sin + q2 * cos], axis=1)
        kr = jnp.concatenate([k1 * cos - k2 * sin, k1 * sin + k2 * cos], axis=1) * (dk ** -0.5)
        qrb = qr.astype(BF16)
        vb = vv[:, sl].astype(BF16)
        inner = _dot_nt(qrb, kr.astype(BF16)) * dmask_ref[h]
        s = s_scr[h]
        o = _dot(inner.astype(BF16), vb) + _dot(qrb, s.astype(BF16)) * qdec_ref[:, h:h + 1]
        kd = (kr * kdec_ref[:, h:h + 1]).astype(BF16)
        s_scr[h] = s * cdec[h] + _dot_tn(kd, vb)
        on = o * lax.rsqrt(jnp.mean(o * o, axis=-1, keepdims=True) + EPS) * retg_ref[:, sl]
        ro_ref[:, sl] = (on * _silu(gr[:, sl]))[:c_in].astype(ro_ref.dtype)

    @pl.when(c == pl.num_programs(1) - 1)
    def _():
        so_ref[0] = s_scr[...]


def _even_mixer(z, s0, ln_g, ln_b, w_s, b_s, ret_g, *, n_batch, seq, pos0, emit_vn, out_dtype):
    rows, n = z.shape
    w = n // 7
    c_in = min(seq, CHUNK)
    n_chunks = seq // c_in
    c_len = CHUNK if seq % CHUNK == 0 else seq
    dmask, qdec, kdec, cdec, ang = _retention_tables(c_len, pos0, n_chunks * CHUNK)
    cos = jnp.asarray(np.cos(ang).astype(np.float32))
    sin = jnp.asarray(np.sin(ang).astype(np.float32))

    def zspec(col):
        return pl.BlockSpec((c_in, w), lambda b, c, col=col: (b * n_chunks + c, col))

    def const(shape):
        return pl.BlockSpec(shape, lambda b, c: (0,) * len(shape))

    row_spec = pl.BlockSpec((c_in, w), lambda b, c: (b * n_chunks + c, 0))
    state_spec = pl.BlockSpec((1,) + s0.shape[1:], lambda b, c: (b, 0, 0, 0))
    out_specs = [row_spec, row_spec, state_spec]
    out_shape = [jax.ShapeDtypeStruct((rows, w), out_dtype),
                 jax.ShapeDtypeStruct((rows, w), out_dtype),
                 jax.ShapeDtypeStruct(s0.shape, F32)]
    if emit_vn:
        out_specs.append(row_spec)
        out_shape.append(jax.ShapeDtypeStruct((rows, w), F32))
    return pl.pallas_call(
        functools.partial(_even_mixer_kernel, c_in=c_in, cdec=cdec, emit_vn=emit_vn),
        grid=(n_batch, n_chunks),
        in_specs=[zspec(i) for i in range(7)] + [
            const((1, w)), const((1, w)), const(w_s.shape), const((CHUNK, A_GROUPS)), const((1, w)),
            pl.BlockSpec((CHUNK, LANES), lambda b, c: (c, 0)),
            pl.BlockSpec((CHUNK, LANES), lambda b, c: (c, 0)),
            const(dmask.shape), const(qdec.shape), const(kdec.shape),
            state_spec,
        ],
        out_specs=out_specs,
        out_shape=out_shape,
        scratch_shapes=[pltpu.VMEM(s0.shape[1:], F32)],
        compiler_params=_cparams("arbitrary", "arbitrary"),
        name="even_mixer",
    )(z, z, z, z, z, z, z, ln_g.reshape(1, w), ln_b.reshape(1, w), w_s, b_s.T, ret_g.reshape(1, w),
      cos, sin, jnp.asarray(dmask), jnp.asarray(qdec), jnp.asarray(kdec), s0)


POOL_PREV = 16
POOL_LEAD = SUBLANES


def _pool_kernel(cin_ref, cg_ref, prev_ref, cnt_ref, wp_ref, ps_ref, o_ref, e0, e1, e2, e3, e4, *, tq, t_pad):
    t = pl.program_id(1)
    base = POOL_LEAD + POOL_PREV
    total = base + t_pad
    w = cin_ref.shape[1]
    gw = w // len(POOL_WINDOWS)

    @pl.when(t == 0)
    def _():
        for e in (e0, e1, e2, e3, e4):
            e[0:POOL_LEAD, :] = jnp.zeros((POOL_LEAD, w), F32)
        e0[POOL_LEAD:base, :] = prev_ref[0]
        if t_pad > tq:
            e0[base + tq:total, :] = jnp.zeros((t_pad - tq, w), F32)

    e0[base:base + tq, :] = cin_ref[...]
    levels = (e0, e1, e2, e3, e4)
    for lvl in range(1, 5):
        shift = 1 << (lvl - 1)
        c0 = (lvl - 1) * gw
        src, dst = levels[lvl - 1], levels[lvl]
        dst[POOL_LEAD:total, c0:] = (src[POOL_LEAD:total, c0:]
                                     + src[POOL_LEAD - shift:total - shift, c0:])
    cur = e0[base:total, :]
    cg = cg_ref[...]
    for gi in range(len(POOL_WINDOWS)):
        sl = slice(gi * gw, (gi + 1) * gw)
        win = levels[gi + 1][base:total, sl]
        pooled = win / cnt_ref[:, gi:gi + 1] - cur[:, sl]
        mixed = _dot(pooled.astype(BF16), wp_ref[0, gi]) * ps_ref[:, sl]
        o_ref[:, sl] = (mixed[:tq] * _silu(cg[:, sl])).astype(o_ref.dtype)
    e0[POOL_LEAD:base, :] = e0[POOL_LEAD + t_pad:base + t_pad, :]


def _pool_branch(z_main, prev, cnt, wp_bf, pool_scale, *, layer_slot, n_batch, seq, tq, out_dtype):
    rows = z_main.shape[0]
    w = wp_bf.shape[1] * wp_bf.shape[2]
    t_pad = max(tq, POOL_PREV)
    n_t = seq // tq
    ext = pltpu.VMEM((POOL_LEAD + POOL_PREV + t_pad, w), F32)
    return pl.pallas_call(
        functools.partial(_pool_kernel, tq=tq, t_pad=t_pad),
        grid=(n_batch, n_t),
        in_specs=[
            pl.BlockSpec((tq, w), lambda b, t: (b * n_t + t, 0)),
            pl.BlockSpec((tq, w), lambda b, t: (b * n_t + t, 1)),
            pl.BlockSpec((1, POOL_PREV, w), lambda b, t: (b, 0, 0)),
            pl.BlockSpec((t_pad, len(POOL_WINDOWS)), lambda b, t: (t, 0)),
            pl.BlockSpec((1,) + wp_bf.shape[1:], lambda b, t: (layer_slot, 0, 0, 0)),
            pl.BlockSpec((1, w), lambda b, t: (0, 0)),
        ],
        out_specs=pl.BlockSpec((tq, w), lambda b, t: (b * n_t + t, 0)),
        out_shape=jax.ShapeDtypeStruct((rows, w), out_dtype),
        scratch_shapes=[ext] * 5,
        compiler_params=_cparams("arbitrary", "arbitrary"),
        name="pool_branch",
    )(z_main, z_main, prev, cnt, wp_bf, pool_scale.reshape(1, w))


def _pool_counts(pos0, n_rows):
    pos = pos0 + np.arange(n_rows)
    return jnp.asarray(np.stack([np.minimum(pos + 1, wd) for wd in POOL_WINDOWS], axis=1).astype(np.float32))


def _split3(x):
    a = x.astype(BF16)
    r1 = x - a.astype(F32)
    b = r1.astype(BF16)
    c = (r1 - b.astype(F32)).astype(BF16)
    return a, b, c


def _fcum_kernel(lf_ref, fc_ref, fr_ref, *, blk):
    n_blk = lf_ref.shape[0] // blk
    tri = (lax.broadcasted_iota(jnp.int32, (blk, blk), 0)
           >= lax.broadcasted_iota(jnp.int32, (blk, blk), 1)).astype(BF16)
    carry = jnp.zeros((1, LANES), F32)
    for i in range(n_blk):
        a, b, c = _split3(lf_ref[i * blk:(i + 1) * blk, :])
        cs = (_dot(tri, a) + _dot(tri, b)) + _dot(tri, c) + carry
        carry = cs[blk - 1:blk, :]
        fc_ref[i * blk:(i + 1) * blk, :] = cs
        fr_ref[0, i] = cs.T[:D_HEADS, :]


def _fcum(lf, *, n_batch, seq, blk):
    return pl.pallas_call(
        functools.partial(_fcum_kernel, blk=blk),
        grid=(n_batch,),
        in_specs=[pl.BlockSpec((seq, LANES), lambda b: (b, 0))],
        out_specs=[pl.BlockSpec((seq, LANES), lambda b: (b, 0)),
                   pl.BlockSpec((1, seq // blk, D_HEADS, blk), lambda b: (b, 0, 0, 0))],
        out_shape=[jax.ShapeDtypeStruct((n_batch * seq, LANES), F32),
                   jax.ShapeDtypeStruct((n_batch, seq // blk, D_HEADS, blk), F32)],
        compiler_params=_cparams("arbitrary"),
        name="forget_cumsum",
    )(lf)


SCORE_LEAD = 2


def _attn_prompt_kernel(q_ref, dg_ref, k_ref, v_ref, fc_ref, fr_ref, o_ref,
                        kb_scr, vt_scr, qt_scr, *acc_scrs, tq):
    i = pl.program_id(1)
    dh = LANES
    scale = dh ** -0.5
    n_blk = k_ref.shape[1] // tq

    @pl.when(i == 0)
    def _():
        kb_scr[...] = k_ref[0].astype(BF16)
        for h in range(D_HEADS):
            for kb in range(n_blk):
                vt_scr[kb, h * dh:(h + 1) * dh, :] = (
                    v_ref[0, kb * tq:(kb + 1) * tq, h * dh:(h + 1) * dh].T.astype(BF16))

    for h in range(D_HEADS):
        qt_scr[h] = q_ref[:, h * dh:(h + 1) * dh].T.astype(BF16)
        acc_scrs[h][...] = jnp.zeros((dh, tq), F32)
    key_le_query = (lax.broadcasted_iota(jnp.int32, (tq, tq), 0)
                    <= lax.broadcasted_iota(jnp.int32, (tq, tq), 1))

    def block(kb, stats, masked):
        r0 = pl.multiple_of(kb * tq, tq)
        new_stats = []

        def scores(h):
            sl = slice(h * dh, (h + 1) * dh)
            return _dot(kb_scr[pl.ds(r0, tq), sl], qt_scr[h]) * scale - fc_ref[pl.ds(r0, tq), h:h + 1]

        pending = [scores(h) for h in range(SCORE_LEAD)]
        for h in range(D_HEADS):
            sl = slice(h * dh, (h + 1) * dh)
            m_old, l_old = stats[h]
            fq = fr_ref[0, 0, h:h + 1, :]
            u = pending.pop(0)
            if h + SCORE_LEAD < D_HEADS:
                pending.append(scores(h + SCORE_LEAD))
            if masked:
                u = jnp.where(key_le_query, u, NEG_BIG)
            m_new = jnp.maximum(m_old, jnp.max(u, axis=0, keepdims=True) + fq)
            alpha = jnp.exp(m_old - m_new)
            p = jnp.exp(u - (m_new - fq))
            new_stats.append((m_new, alpha * l_old + jnp.sum(p, axis=0, keepdims=True)))
            acc_scrs[h][...] = alpha * acc_scrs[h][...] + _dot(vt_scr[kb, sl, :], p.astype(BF16))
        return tuple(new_stats)

    stats0 = tuple((jnp.full((1, tq), NEG_BIG, F32), jnp.zeros((1, tq), F32)) for _ in range(D_HEADS))
    stats = lax.fori_loop(0, i, lambda kb, st: block(kb, st, False), stats0)
    stats = block(i, stats, True)
    for h in range(D_HEADS):
        sl = slice(h * dh, (h + 1) * dh)
        o_ref[:, sl] = ((acc_scrs[h][...] / stats[h][1]).T * _silu(dg_ref[:, sl])).astype(o_ref.dtype)


def _attn_prompt(z_main, k_all, v_all, fcol, frow, *, layer_slot, n_batch, seq, tq, out_dtype):
    rows = z_main.shape[0]
    w = k_all.shape[2]
    n_q = seq // tq
    return pl.pallas_call(
        functools.partial(_attn_prompt_kernel, tq=tq),
        grid=(n_batch, n_q),
        in_specs=[
            pl.BlockSpec((tq, w), lambda b, i: (b * n_q + i, 2)),
            pl.BlockSpec((tq, w), lambda b, i: (b * n_q + i, 3)),
            pl.BlockSpec((1, seq, w), lambda b, i: (layer_slot, b, 0)),
            pl.BlockSpec((1, seq, w), lambda b, i: (layer_slot, b, 0)),
            pl.BlockSpec((seq, LANES), lambda b, i: (b, 0)),
            pl.BlockSpec((1, 1, D_HEADS, tq), lambda b, i: (b, i, 0, 0)),
        ],
        out_specs=pl.BlockSpec((tq, w), lambda b, i: (b * n_q + i, 0)),
        out_shape=jax.ShapeDtypeStruct((rows, w), out_dtype),
        scratch_shapes=[pltpu.VMEM((seq, w), BF16), pltpu.VMEM((n_q, w, tq), BF16),
                        pltpu.VMEM((D_HEADS, LANES, tq), BF16)]
        + [pltpu.VMEM((LANES, tq), F32)] * D_HEADS,
        compiler_params=_cparams("arbitrary", "arbitrary"),
        name="attn_prompt",
    )(z_main, z_main, k_all, v_all, fcol, frow)


PAGES_PER_STEP = 8


def _attn_sample_tables(n_new, g_pages):
    pg = np.arange(LANES)
    same_head = (pg[:, None] % D_HEADS) == (pg[None, :] % D_HEADS)
    later = (pg[:, None] // D_HEADS) > (pg[None, :] // D_HEADS)
    c_later = (same_head & later).astype(np.float32)
    c_same = same_head.astype(np.float32)
    rows = np.arange(g_pages * SUBLANES)
    r_later = (rows[None, :] > rows[:, None]).astype(np.float32)
    r = np.arange(D_HEADS * n_new)
    past_ok = (pg[None, :] % D_HEADS) == (r[:, None] // n_new)
    past_bias = np.where(past_ok, 0.0, NEG_BIG).astype(np.float32)
    cn = np.arange(n_new * D_HEADS)
    new_ok = ((cn[None, :] % D_HEADS) == (r[:, None] // n_new)) & ((cn[None, :] // D_HEADS) <= (r[:, None] % n_new))
    new_bias = np.where(new_ok, 0.0, NEG_BIG).astype(np.float32)
    m_cols = ((cn[:, None] % D_HEADS == cn[None, :] % D_HEADS)
              & (cn[:, None] // D_HEADS <= cn[None, :] // D_HEADS)).astype(np.float32)
    m_rows = ((cn[None, :] % D_HEADS == r[:, None] // n_new)
              & (cn[None, :] // D_HEADS <= r[:, None] % n_new)).astype(np.float32)
    return c_later, c_same, r_later, past_bias, new_bias, m_cols, m_rows


def _page_copies(pt_ref, ck_hbm, cv_hbm, clf_hbm, kbuf, vbuf, lfbuf, sems, step, slot, *,
                 layer_slot, n_groups, g_pages):
    b = step // n_groups
    first_page = (n_groups - 1 - step % n_groups) * g_pages
    copies = []
    for g in range(g_pages):
        page = pt_ref[b, first_page + g]
        copies.append(pltpu.make_async_copy(ck_hbm.at[layer_slot, page], kbuf.at[slot, g], sems.at[0, slot]))
        copies.append(pltpu.make_async_copy(cv_hbm.at[layer_slot, page], vbuf.at[slot, g], sems.at[1, slot]))
        copies.append(pltpu.make_async_copy(clf_hbm.at[layer_slot, page], lfbuf.at[slot, g], sems.at[2, slot]))
    return copies


def _attn_sample_kernel(pt_ref, q_ref, kn_ref, vn_ref, lfr_ref, lfc_ref, dg_ref,
                        cl_ref, cs_ref, rl_ref, pb_ref, nb_ref, mc_ref, mr_ref,
                        ck_hbm, cv_hbm, clf_hbm, o_ref,
                        kbuf, vbuf, lfbuf, sems, m_scr, l_scr, acc_scr, run_scr, base_scr, *,
                        layer_slot, n_groups, g_pages):
    step = pl.program_id(0)
    n_steps = pl.num_programs(0)
    slot = step % 2
    grp = step % n_groups
    scale = LANES ** -0.5
    hi = lax.Precision.HIGHEST
    copies = functools.partial(_page_copies, pt_ref, ck_hbm, cv_hbm, clf_hbm, kbuf, vbuf, lfbuf, sems,
                               layer_slot=layer_slot, n_groups=n_groups, g_pages=g_pages)

    @pl.when(step == 0)
    def _():
        for c in copies(0, 0):
            c.start()

    @pl.when(step + 1 < n_steps)
    def _():
        for c in copies(step + 1, 1 - slot):
            c.start()

    qb = q_ref[0].astype(BF16)
    rowc = jnp.sum(mr_ref[...] * lfr_ref[0], axis=-1, keepdims=True)

    @pl.when(grp == 0)
    def _():
        m_scr[...] = jnp.full(m_scr.shape, NEG_BIG, F32)
        l_scr[...] = jnp.zeros(l_scr.shape, F32)
        acc_scr[...] = jnp.zeros(acc_scr.shape, F32)
        run_scr[...] = jnp.zeros(run_scr.shape, F32)
        base_scr[...] = rowc + pb_ref[...]

    for c in copies(step, slot):
        c.wait()

    n_rows = g_pages * SUBLANES
    lf = lfbuf[slot].reshape(n_rows, LANES)
    within = jnp.dot(lf, cl_ref[...], precision=hi, preferred_element_type=F32)
    rowtot = jnp.dot(lf, cs_ref[...], precision=hi, preferred_element_type=F32)
    later_rows = jnp.dot(rl_ref[...], rowtot, precision=hi, preferred_element_type=F32)
    g_past = (within + later_rows) + run_scr[...]
    run_scr[...] = run_scr[...] + jnp.sum(rowtot, axis=0, keepdims=True)

    n_keys = g_pages * PAGE_SIZE * D_HEADS
    s_all = _dot_nt(qb, kbuf[slot].reshape(n_keys, LANES).astype(BF16)) * scale
    base = base_scr[...]
    blocks = []
    s_max = None
    for j in range(n_rows):
        sj = (s_all[:, j * LANES:(j + 1) * LANES] + base) + g_past[j:j + 1, :]
        blocks.append(sj)
        s_max = sj if s_max is None else jnp.maximum(s_max, sj)
    m_old = m_scr[...]
    m_new = jnp.maximum(m_old, jnp.max(s_max, axis=-1, keepdims=True))
    alpha = jnp.exp(m_old - m_new)
    p_sum = None
    p_blocks = []
    for sj in blocks:
        pj = jnp.exp(sj - m_new)
        p_sum = pj if p_sum is None else p_sum + pj
        p_blocks.append(pj.astype(BF16))
    l_scr[...] = alpha * l_scr[...] + jnp.sum(p_sum, axis=-1, keepdims=True)
    acc_scr[...] = alpha * acc_scr[...] + _dot(jnp.concatenate(p_blocks, axis=1),
                                               vbuf[slot].reshape(n_keys, LANES).astype(BF16))
    m_scr[...] = m_new

    @pl.when(grp == n_groups - 1)
    def _():
        c_new = jnp.sum(mc_ref[...] * lfc_ref[0], axis=0, keepdims=True)
        s = (_dot_nt(qb, kn_ref[0].astype(BF16)) * scale + rowc) - c_new + nb_ref[...]
        m_old = m_scr[...]
        m_new = jnp.maximum(m_old, jnp.max(s, axis=-1, keepdims=True))
        alpha = jnp.exp(m_old - m_new)
        pr = jnp.exp(s - m_new)
        l_fin = alpha * l_scr[...] + jnp.sum(pr, axis=-1, keepdims=True)
        acc = alpha * acc_scr[...] + _dot(pr.astype(BF16), vn_ref[0].astype(BF16))
        o_ref[0] = (acc / l_fin) * _silu(dg_ref[0])


def _attn_sample(q_hq, kn, vn, lf_row, lf_col, dg_hq, cache_k, cache_v, cache_lf, page_table, *, layer_slot):
    bd, r, dh = q_hq.shape
    n_pages = page_table.shape[1]
    n_new = r // D_HEADS
    n_slots, n_phys = cache_k.shape[:2]
    pg_rows = PAGE_SIZE * D_HEADS
    ck = cache_k.reshape(n_slots, n_phys, pg_rows, dh)
    cv = cache_v.reshape(n_slots, n_phys, pg_rows, dh)
    clf = cache_lf.reshape(n_slots, n_phys, SUBLANES, LANES)
    g_pages = min(PAGES_PER_STEP, n_pages)
    assert n_pages % g_pages == 0
    n_groups = n_pages // g_pages
    tabs = [jnp.asarray(t) for t in _attn_sample_tables(n_new, g_pages)]

    def per_b(shape):
        return pl.BlockSpec((1,) + shape, lambda s, pt: (s // n_groups, 0, 0))

    def const(t):
        return pl.BlockSpec(t.shape, lambda s, pt: (0, 0))

    hbm = pl.BlockSpec(memory_space=pl.ANY)
    grid_spec = pltpu.PrefetchScalarGridSpec(
        num_scalar_prefetch=1,
        grid=(bd * n_groups,),
        in_specs=[per_b((r, dh)), per_b((r, dh)), per_b((r, dh)), per_b((1, r)), per_b((r, 1)), per_b((r, dh))]
        + [const(t) for t in tabs] + [hbm, hbm, hbm],
        out_specs=per_b((r, dh)),
        scratch_shapes=[
            pltpu.VMEM((2, g_pages, pg_rows, dh), F32), pltpu.VMEM((2, g_pages, pg_rows, dh), F32),
            pltpu.VMEM((2, g_pages, SUBLANES, LANES), F32), pltpu.SemaphoreType.DMA((3, 2)),
            pltpu.VMEM((r, 1), F32), pltpu.VMEM((r, 1), F32), pltpu.VMEM((r, dh), F32),
            pltpu.VMEM((1, LANES), F32), pltpu.VMEM((r, LANES), F32)],
    )
    return pl.pallas_call(
        functools.partial(_attn_sample_kernel, layer_slot=layer_slot, n_groups=n_groups, g_pages=g_pages),
        grid_spec=grid_spec,
        out_shape=jax.ShapeDtypeStruct((bd, r, dh), F32),
        compiler_params=_cparams("arbitrary"),
        name="attn_sample",
    )(page_table, q_hq, kn, vn, lf_row, lf_col, dg_hq, *tabs, ck, cv, clf)


TM_PROMPT = 1024
TM_OUT = 512
TQ_POOL = 256
TQ_ATTN = 256


def kernel(x_prompt, x_sample, c_prompt, c_sample, state_ret, state_pool, cache_k, cache_v, cache_logf,
           page_table, g_pre, g_post, w_ada, b_ada, w_in_even, w_out_even, ln_a_g, ln_a_b, w_s, b_s,
           ret_g, w_in_odd, b_f, w_out_odd, w_pool, pool_scale):
    bp, seq, d = x_prompt.shape
    bd, n_new, _ = x_sample.shape
    depth = g_pre.shape[0]
    n_odd = w_in_odd.shape[0]
    w = d // 2
    n_past = page_table.shape[1] * PAGE_SIZE
    rows_s = bd * n_new

    c_all = jnp.concatenate([c_prompt, c_sample], axis=0)
    c_all = jnp.pad(c_all, ((0, -c_all.shape[0] % (2 * SUBLANES)), (0, 0)))
    mod = _ada_mod(c_all, w_ada, b_ada)

    def mods(l):
        mp = mod[l, :bp].reshape(bp, 1, 3 * d)
        ms = jnp.repeat(mod[l, bp:bp + bd], n_new, axis=0).reshape(1, rows_s, 3 * d)
        return [(m[..., :d], m[..., d:2 * d], m[..., 2 * d:]) for m in (mp, ms)]

    xp = x_prompt.reshape(bp * seq, d)
    xs = x_sample.reshape(rows_s, d)
    ret_p, ret_s, gv_s, pool_p, pool_s, lfp_l, lfs_l = [], [], [], [], [], [], []
    kv_p = kv_s = None
    zeros_state = jnp.zeros((bp,) + state_ret.shape[2:], F32)
    w_in_even_bf, w_out_even_bf = w_in_even.astype(BF16), w_out_even.astype(BF16)
    w_in_odd_bf, w_out_odd_bf, w_pool_bf = w_in_odd.astype(BF16), w_out_odd.astype(BF16), w_pool.astype(BF16)
    for l in range(depth):
        j = l // 2
        (shift_p, scale_p, gate_p), (shift_s, scale_s, gate_s) = mods(l)
        gpre = g_pre[l].reshape(1, d)
        gpost = g_post[l].reshape(1, d)
        if l % 2 == 0:
            w_out = w_out_even_bf
            zp = _inproj_even(xp, scale_p, shift_p, gpre, w_in_even_bf, layer_slot=j, per_row=False,
                              rows_per_mod=seq, tm=TM_PROMPT)
            zs = _inproj_even(xs, scale_s, shift_s, gpre, w_in_even_bf, layer_slot=j, per_row=True,
                              rows_per_mod=1, tm=rows_s)
            mix = functools.partial(_even_mixer, ln_g=ln_a_g[j], ln_b=ln_a_b[j], w_s=w_s[j], b_s=b_s[j],
                                    ret_g=ret_g[j])
            ap, rp, sp = mix(zp, zeros_state, n_batch=bp, seq=seq, pos0=0, emit_vn=False, out_dtype=BF16)
            as_, rs, ss, vn_s = mix(zs, state_ret[j], n_batch=bd, seq=n_new, pos0=n_past, emit_vn=True,
                                    out_dtype=F32)
            ret_p.append(sp)
            ret_s.append(ss)
            gv_s.append(vn_s.reshape(bd, n_new, w))
        else:
            w_in = w_in_odd_bf
            wf = jnp.pad(w_in_odd[j, :, 6 * w:], ((0, 0), (0, LANES - D_HEADS))).astype(BF16)
            bf = jnp.pad(b_f[j], (0, LANES - D_HEADS)).reshape(1, LANES)
            w_out = w_out_odd_bf
            wp = w_pool_bf
            zp, kp_all, vp_all, lfp = _inproj_odd(xp, scale_p, shift_p, gpre, w_in, wf, bf, kv_p,
                                                  layer_slot=j, n_slots=n_odd, per_row=False,
                                                  rows_per_mod=seq, tm=TM_PROMPT)
            kv_p = (kp_all, vp_all)
            zs, ks_all, vs_all, lfs = _inproj_odd(xs, scale_s, shift_s, gpre, w_in, wf, bf, kv_s,
                                                  layer_slot=j, n_slots=n_odd, per_row=True,
                                                  rows_per_mod=1, tm=rows_s)
            kv_s = (ks_all, vs_all)
            ap = _pool_branch(zp, jnp.zeros((bp, POOL_PREV, w), F32), _pool_counts(0, seq), wp, pool_scale[j],
                              layer_slot=j, n_batch=bp, seq=seq, tq=TQ_POOL, out_dtype=BF16)
            fcol, frow = _fcum(lfp, n_batch=bp, seq=seq, blk=TQ_ATTN)
            rp = _attn_prompt(zp, kp_all, vp_all, fcol, frow, layer_slot=j, n_batch=bp, seq=seq, tq=TQ_ATTN,
                              out_dtype=BF16)
            pool_p.append(zp.reshape(bp, seq, 4 * w)[:, seq - POOL_BUF:, :w])
            lfp_l.append(lfp[:, :D_HEADS].reshape(bp, seq, D_HEADS))
            prev = jnp.pad(state_pool[j], ((0, 0), (POOL_PREV - POOL_BUF, 0), (0, 0)))
            as_ = _pool_branch(zs, prev, _pool_counts(n_past, POOL_PREV), wp, pool_scale[j],
                               layer_slot=j, n_batch=bd, seq=n_new, tq=n_new, out_dtype=F32)
            zs4 = zs.reshape(bd, n_new, 4, D_HEADS, LANES)
            to_hq = lambda a: a.transpose(0, 2, 1, 3).reshape(bd, D_HEADS * n_new, LANES)
            lf_new = lfs[:, :D_HEADS].reshape(bd, n_new * D_HEADS)
            o_hq = _attn_sample(to_hq(zs4[:, :, 2]), ks_all[j].reshape(bd, n_new * D_HEADS, LANES),
                                vs_all[j].reshape(bd, n_new * D_HEADS, LANES),
                                lf_new.reshape(bd, 1, -1), lf_new.reshape(bd, -1, 1), to_hq(zs4[:, :, 3]),
                                cache_k, cache_v, cache_logf, page_table, layer_slot=j)
            rs = o_hq.reshape(bd, D_HEADS, n_new, LANES).transpose(0, 2, 1, 3).reshape(rows_s, w)
            c_in_s = zs[:, :w].reshape(bd, n_new, w)
            pool_s.append(jnp.concatenate([state_pool[j], c_in_s], axis=1)[:, -POOL_BUF:])
            lfs_l.append(lfs[:, :D_HEADS].reshape(bd, n_new, D_HEADS))
        xp = _outproj(ap, rp, xp, gate_p, gpost, w_out, layer_slot=j, per_row=False, rows_per_mod=seq, tm=TM_OUT)
        xs = _outproj(as_, rs, xs, gate_s, gpost, w_out, layer_slot=j, per_row=True, rows_per_mod=1, tm=rows_s)

    dh = LANES
    return (xp.reshape(bp, seq, d), xs.reshape(bd, n_new, d),
            jnp.stack(ret_p), jnp.stack(ret_s), jnp.stack(gv_s), jnp.stack(pool_p), jnp.stack(pool_s),
            kv_p[0].reshape(n_odd, bp, seq, D_HEADS, dh), kv_p[1].reshape(n_odd, bp, seq, D_HEADS, dh),
            jnp.stack(lfp_l),
            kv_s[0].reshape(n_odd, bd, n_new, D_HEADS, dh), kv_s[1].reshape(n_odd, bd, n_new, D_HEADS, dh),
            jnp.stack(lfs_l))
```

```python
import functools

import numpy as np
import jax
import jax.numpy as jnp
from jax import lax
from jax.experimental import pallas as pl
from jax.experimental.pallas import tpu as pltpu

F32 = jnp.float32
BF16 = jnp.bfloat16

EPS = 1e-6
ROPE_BASE = 10000.0
CHUNK = 128
A_GROUPS = 8
B_HEADS = 4
POOL_WINDOWS = (2, 4, 8, 16)
POOL_BUF = 15
D_HEADS = 8
PAGE_SIZE = 128
LANES = 128
SUBLANES = 8
NEG_BIG = -1e30
VMEM_LIMIT = 56 * 1024 * 1024


def _cparams(*sem):
    return pltpu.CompilerParams(dimension_semantics=sem, vmem_limit_bytes=VMEM_LIMIT)


def _silu(x):
    return x * jax.nn.sigmoid(x)


def _log_sigmoid(x):
    return jnp.minimum(x, 0.0) - jnp.log1p(jnp.exp(-jnp.abs(x)))


def _dot(a, b):
    return jnp.dot(a, b, preferred_element_type=F32)


def _dot_nt(a, b):
    return lax.dot_general(a, b, (((1,), (1,)), ((), ())), preferred_element_type=F32)


def _dot_tn(a, b):
    return lax.dot_general(a, b, (((0,), (0,)), ((), ())), preferred_element_type=F32)


def _ada_kernel(c_ref, w_ref, b_ref, o_ref):
    a = _silu(c_ref[...]).astype(BF16)
    o_ref[0] = _dot(a, w_ref[0].astype(BF16)) + b_ref[0]


def _ada_mod(c_all, w_ada, b_ada, tn=1024):
    depth, d, n = w_ada.shape
    r = c_all.shape[0]
    return pl.pallas_call(
        _ada_kernel,
        grid=(depth, n // tn),
        in_specs=[
            pl.BlockSpec((r, d), lambda l, j: (0, 0)),
            pl.BlockSpec((1, d, tn), lambda l, j: (l, 0, j)),
            pl.BlockSpec((1, 1, tn), lambda l, j: (l, 0, j)),
        ],
        out_specs=pl.BlockSpec((1, r, tn), lambda l, j: (l, 0, j)),
        out_shape=jax.ShapeDtypeStruct((depth, r, n), F32),
        compiler_params=_cparams("arbitrary", "arbitrary"),
        name="ada_mod",
    )(c_all, w_ada, b_ada.reshape(depth, 1, n))


def _modnorm(x, scale, shift, g):
    y = x * lax.rsqrt(jnp.mean(x * x, axis=-1, keepdims=True) + EPS) * g
    return (y * (1.0 + scale) + shift).astype(BF16)


def _modnorm_kernel(x_ref, sc_ref, sh_ref, g_ref, h_ref):
    h_ref[...] = _modnorm(x_ref[...], sc_ref[0], sh_ref[0], g_ref[...])


def _mod_spec(per_row, rows_per_mod, tm, d):
    if per_row:
        return pl.BlockSpec((1, tm, d), lambda i: (0, i, 0))
    return pl.BlockSpec((1, 1, d), lambda i: ((i * tm) // rows_per_mod, 0, 0))


def _modnorm_call(x2d, scale, shift, g, *, per_row, rows_per_mod, tm):
    rows, d = x2d.shape
    return pl.pallas_call(
        _modnorm_kernel,
        grid=(rows // tm,),
        in_specs=[pl.BlockSpec((tm, d), lambda i: (i, 0)),
                  _mod_spec(per_row, rows_per_mod, tm, d), _mod_spec(per_row, rows_per_mod, tm, d),
                  pl.BlockSpec((1, d), lambda i: (0, 0))],
        out_specs=pl.BlockSpec((tm, d), lambda i: (i, 0)),
        out_shape=jax.ShapeDtypeStruct((rows, d), BF16),
        compiler_params=_cparams("arbitrary"),
        name="modnorm",
    )(x2d, scale, shift, g)


def _inproj_even_kernel(h_ref, w_ref, z_ref):
    z_ref[...] = _dot(h_ref[...], w_ref[0].astype(BF16))


def _inproj_odd_kernel(h_ref, w_ref, wf_ref, bf_ref, *rest, tiles_per_group, kv_slot):
    z_ref, k_ref, v_ref, lf_ref = rest[-4:]
    j = pl.program_id(1)

    def project():
        return _dot(h_ref[...], w_ref[0].astype(BF16))

    def put(kv_ref):
        for s in range(kv_ref.shape[0]):
            kv_ref[s] = project() if s == kv_slot else jnp.zeros(kv_ref.shape[1:], F32)

    @pl.when(j == 0)
    def _():
        lf_ref[...] = _log_sigmoid(_dot(h_ref[...], wf_ref[...]) + bf_ref[...])

    group = j // tiles_per_group

    @pl.when(jnp.logical_or(group < 3, group == 5))
    def _():
        z_ref[...] = project()

    @pl.when(group == 3)
    def _():
        put(k_ref)

    @pl.when(group == 4)
    def _():
        put(v_ref)


def _inproj_even(h, w_all, *, layer_slot, tm, tn):
    rows, d = h.shape
    n = w_all.shape[2]
    return pl.pallas_call(
        _inproj_even_kernel,
        grid=(rows // tm, n // tn),
        in_specs=[
            pl.BlockSpec((tm, d), lambda i, j: (i, 0)),
            pl.BlockSpec((1, d, tn), lambda i, j: (layer_slot, 0, j)),
        ],
        out_specs=pl.BlockSpec((tm, tn), lambda i, j: (i, j)),
        out_shape=jax.ShapeDtypeStruct((rows, n), F32),
        compiler_params=_cparams("arbitrary", "arbitrary"),
        name="inproj_even",
    )(h, w_all)


def _inproj_odd(h, w_all, wf_bf, bf, kv_prev, *, layer_slot, n_slots, tm, tn):
    rows, d = h.shape
    w = (w_all.shape[2] - D_HEADS) // 6
    tpg = w // tn
    n_tiles = 6 * tpg

    creates_kv = kv_prev is None
    kv_block = (n_slots if creates_kv else 1, tm, tn)
    kv_first = 0 if creates_kv else layer_slot

    def zmap(i, j):
        return (i, j - jnp.clip(j - (3 * tpg - 1), 0, 2 * tpg))

    def kmap(i, j):
        return (kv_first, i, jnp.clip(j - 3 * tpg, 0, tpg - 1))

    def vmap(i, j):
        return (kv_first, i, jnp.clip(j - 4 * tpg, 0, tpg - 1))

    in_specs = [
        pl.BlockSpec((tm, d), lambda i, j: (i, 0)),
        pl.BlockSpec((1, d, tn), lambda i, j: (layer_slot, 0, j)),
        pl.BlockSpec((d, LANES), lambda i, j: (0, 0)),
        pl.BlockSpec((1, LANES), lambda i, j: (0, 0)),
    ]
    args = [h, w_all, wf_bf, bf]
    aliases = {}
    if kv_prev is not None:
        in_specs += [pl.BlockSpec(memory_space=pl.ANY)] * 2
        aliases = {len(args): 1, len(args) + 1: 2}
        args += list(kv_prev)
    kv_shape = jax.ShapeDtypeStruct((n_slots, rows, w), F32)
    return pl.pallas_call(
        functools.partial(_inproj_odd_kernel, tiles_per_group=tpg, kv_slot=layer_slot if creates_kv else 0),
        grid=(rows // tm, n_tiles),
        in_specs=in_specs,
        out_specs=[
            pl.BlockSpec((tm, tn), zmap),
            pl.BlockSpec(kv_block, kmap),
            pl.BlockSpec(kv_block, vmap),
            pl.BlockSpec((tm, LANES), lambda i, j: (i, 0)),
        ],
        out_shape=[
            jax.ShapeDtypeStruct((rows, 4 * w), F32),
            kv_shape,
            kv_shape,
            jax.ShapeDtypeStruct((rows, LANES), F32),
        ],
        input_output_aliases=aliases,
        compiler_params=_cparams("arbitrary", "arbitrary"),
        name="inproj_odd",
    )(*args)


def _outproj_kernel(a_ref, b_ref, x_ref, gate_ref, g_ref, w1_ref, w2_ref, *rest, with_next):
    y = _dot(a_ref[...].astype(BF16), w1_ref[0]) + _dot(b_ref[...].astype(BF16), w2_ref[0])
    yn = y * lax.rsqrt(jnp.mean(y * y, axis=-1, keepdims=True) + EPS) * g_ref[...]
    x_new = x_ref[...] + gate_ref[0] * yn
    if with_next:
        sc_ref, sh_ref, gn_ref, o_ref, h_ref = rest
        h_ref[...] = _modnorm(x_new, sc_ref[0], sh_ref[0], gn_ref[...])
    else:
        o_ref, = rest
    o_ref[...] = x_new


def _outproj(a, b, x2d, gate, g, w_bf, next_mod, *, layer_slot, per_row, rows_per_mod, tm):
    rows, d = x2d.shape
    w = a.shape[1]
    mod_spec = _mod_spec(per_row, rows_per_mod, tm, d)
    row_spec = pl.BlockSpec((tm, d), lambda i: (i, 0))
    vec_spec = pl.BlockSpec((1, d), lambda i: (0, 0))
    in_specs = [
        pl.BlockSpec((tm, w), lambda i: (i, 0)),
        pl.BlockSpec((tm, w), lambda i: (i, 0)),
        row_spec, mod_spec, vec_spec,
        pl.BlockSpec((1, w, d), lambda i: (layer_slot, 0, 0)),
        pl.BlockSpec((1, w, d), lambda i: (layer_slot, 1, 0)),
    ]
    args = [a, b, x2d, gate, g, w_bf, w_bf]
    out_specs = [row_spec]
    out_shape = [jax.ShapeDtypeStruct((rows, d), F32)]
    if next_mod is not None:
        in_specs += [mod_spec, mod_spec, vec_spec]
        args += list(next_mod)
        out_specs.append(row_spec)
        out_shape.append(jax.ShapeDtypeStruct((rows, d), BF16))
    res = pl.pallas_call(
        functools.partial(_outproj_kernel, with_next=next_mod is not None),
        grid=(rows // tm,),
        in_specs=in_specs,
        out_specs=out_specs,
        out_shape=out_shape,
        compiler_params=_cparams("arbitrary"),
        name="outproj",
    )(*args)
    return res if next_mod is not None else (res[0], None)


def _retention_tables(c_len, pos0, n_rows):
    lg = np.log(1.0 - 2.0 ** (-5.0 - np.arange(B_HEADS, dtype=np.float64)))
    t = np.arange(CHUNK, dtype=np.float64)
    diff = t[:, None] - t[None, :]
    dmask = np.where(diff >= 0, np.exp(lg[:, None, None] * np.maximum(diff, 0.0)), 0.0).astype(np.float32)
    qdec = np.exp(lg[None, :] * (t + 1.0)[:, None]).astype(np.float32)
    kdec = np.exp(lg[None, :] * np.maximum(c_len - 1.0 - t, 0.0)[:, None]).astype(np.float32)
    cdec = tuple(float(v) for v in np.exp(lg * c_len).astype(np.float32))
    half = LANES
    inv = ROPE_BASE ** (-np.arange(half, dtype=np.float64) / half)
    pos = (pos0 + np.arange(n_rows)).astype(np.float64)
    ang = pos[:, None] * inv[None, :]
    return dmask, qdec, kdec, cdec, ang


def _even_mixer_kernel(u_ref, v_ref, ga_ref, q_ref, k_ref, vv_ref, gr_ref,
                       lng_ref, lnb_ref, ws_ref, bst_ref, retg_ref, cos_ref, sin_ref,
                       dmask_ref, qdec_ref, kdec_ref, s0_ref, *rest, c_in, cdec, emit_vn):
    if emit_vn:
        ao_ref, ro_ref, so_ref, vn_ref, s_scr = rest
    else:
        ao_ref, ro_ref, so_ref, s_scr = rest
    c = pl.program_id(1)

    @pl.when(c == 0)
    def _():
        s_scr[...] = s0_ref[0]

    def ld(ref):
        x = ref[...]
        if c_in == CHUNK:
            return x
        return jnp.concatenate([x, jnp.zeros((CHUNK - c_in, x.shape[1]), F32)], axis=0)

    v = ld(v_ref)
    xc = v - jnp.mean(v, axis=-1, keepdims=True)
    vn = xc * lax.rsqrt(jnp.mean(xc * xc, axis=-1, keepdims=True) + EPS) * lng_ref[...] + lnb_ref[...]
    if emit_vn:
        vn_ref[...] = vn[:c_in]
    vnb = vn.astype(BF16)
    u = ld(u_ref)
    ga = ld(ga_ref)
    causal = (lax.broadcasted_iota(jnp.int32, (CHUNK, CHUNK), 0)
              >= lax.broadcasted_iota(jnp.int32, (CHUNK, CHUNK), 1))
    gw = u.shape[1] // A_GROUPS
    for g in range(A_GROUPS):
        sl = slice(g * gw, (g + 1) * gw)
        wm = jnp.where(causal, ws_ref[g], 0.0).astype(BF16)
        mixed = _dot(wm, vnb[:, sl]) + bst_ref[:, g:g + 1]
        ao_ref[:, sl] = (u[:, sl] * mixed * _silu(ga[:, sl]))[:c_in].astype(ao_ref.dtype)

    cos = cos_ref[...]
    sin = sin_ref[...]
    q = ld(q_ref)
    k = ld(k_ref)
    vv = ld(vv_ref)
    gr = ld(gr_ref)
    dk = q.shape[1] // B_HEADS
    half = dk // 2
    for h in range(B_HEADS):
        sl = slice(h * dk, (h + 1) * dk)
        q1, q2 = q[:, h * dk:h * dk + half], q[:, h * dk + half:(h + 1) * dk]
        k1, k2 = k[:, h * dk:h * dk + half], k[:, h * dk + half:(h + 1) * dk]
        qr = jnp.concatenate([q1 * cos - q2 * sin, q1 * sin + q2 * cos], axis=1)
        kr = jnp.concatenate([k1 * cos - k2 * sin, k1 * sin + k2 * cos], axis=1) * (dk ** -0.5)
        qrb = qr.astype(BF16)
        vb = vv[:, sl].astype(BF16)
        inner = _dot_nt(qrb, kr.astype(BF16)) * dmask_ref[h]
        s = s_scr[h]
        o = _dot(inner.astype(BF16), vb) + _dot(qrb, s.astype(BF16)) * qdec_ref[:, h:h + 1]
        kd = (kr * kdec_ref[:, h:h + 1]).astype(BF16)
        s_scr[h] = s * cdec[h] + _dot_tn(kd, vb)
        on = o * lax.rsqrt(jnp.mean(o * o, axis=-1, keepdims=True) + EPS) * retg_ref[:, sl]
        ro_ref[:, sl] = (on * _silu(gr[:, sl]))[:c_in].astype(ro_ref.dtype)

    @pl.when(c == pl.num_programs(1) - 1)
    def _():
        so_ref[0] = s_scr[...]


def _even_mixer(z, s0, ln_g, ln_b, w_s, b_s, ret_g, *, n_batch, seq, pos0, emit_vn, out_dtype):
    rows, n = z.shape
    w = n // 7
    c_in = min(seq, CHUNK)
    n_chunks = seq // c_in
    c_len = CHUNK if seq % CHUNK == 0 else seq
    dmask, qdec, kdec, cdec, ang = _retention_tables(c_len, pos0, n_chunks * CHUNK)
    cos = jnp.asarray(np.cos(ang).astype(np.float32))
    sin = jnp.asarray(np.sin(ang).astype(np.float32))

    def zspec(col):
        return pl.BlockSpec((c_in, w), lambda b, c, col=col: (b * n_chunks + c, col))

    def const(shape):
        return pl.BlockSpec(shape, lambda b, c: (0,) * len(shape))

    row_spec = pl.BlockSpec((c_in, w), lambda b, c: (b * n_chunks + c, 0))
    state_spec = pl.BlockSpec((1,) + s0.shape[1:], lambda b, c: (b, 0, 0, 0))
    out_specs = [row_spec, row_spec, state_spec]
    out_shape = [jax.ShapeDtypeStruct((rows, w), out_dtype),
                 jax.ShapeDtypeStruct((rows, w), out_dtype),
                 jax.ShapeDtypeStruct(s0.shape, F32)]
    if emit_vn:
        out_specs.append(row_spec)
        out_shape.append(jax.ShapeDtypeStruct((rows, w), F32))
    return pl.pallas_call(
        functools.partial(_even_mixer_kernel, c_in=c_in, cdec=cdec, emit_vn=emit_vn),
        grid=(n_batch, n_chunks),
        in_specs=[zspec(i) for i in range(7)] + [
            const((1, w)), const((1, w)), const(w_s.shape), const((CHUNK, A_GROUPS)), const((1, w)),
            pl.BlockSpec((CHUNK, LANES), lambda b, c: (c, 0)),
            pl.BlockSpec((CHUNK, LANES), lambda b, c: (c, 0)),
            const(dmask.shape), const(qdec.shape), const(kdec.shape),
            state_spec,
        ],
        out_specs=out_specs,
        out_shape=out_shape,
        scratch_shapes=[pltpu.VMEM(s0.shape[1:], F32)],
        compiler_params=_cparams("arbitrary", "arbitrary"),
        name="even_mixer",
    )(z, z, z, z, z, z, z, ln_g.reshape(1, w), ln_b.reshape(1, w), w_s, b_s.T, ret_g.reshape(1, w),
      cos, sin, jnp.asarray(dmask), jnp.asarray(qdec), jnp.asarray(kdec), s0)


POOL_PREV = 16
POOL_LEAD = SUBLANES


def _pool_kernel(cin_ref, cg_ref, prev_ref, cnt_ref, wp_ref, ps_ref, o_ref, e0, e1, e2, e3, e4, *, tq, t_pad):
    t = pl.program_id(1)
    base = POOL_LEAD + POOL_PREV
    total = base + t_pad
    w = cin_ref.shape[1]
    gw = w // len(POOL_WINDOWS)

    @pl.when(t == 0)
    def _():
        for e in (e0, e1, e2, e3, e4):
            e[0:POOL_LEAD, :] = jnp.zeros((POOL_LEAD, w), F32)
        e0[POOL_LEAD:base, :] = prev_ref[0]
        if t_pad > tq:
            e0[base + tq:total, :] = jnp.zeros((t_pad - tq, w), F32)

    e0[base:base + tq, :] = cin_ref[...]
    levels = (e0, e1, e2, e3, e4)
    for lvl in range(1, 5):
        shift = 1 << (lvl - 1)
        c0 = (lvl - 1) * gw
        src, dst = levels[lvl - 1], levels[lvl]
        dst[POOL_LEAD:total, c0:] = (src[POOL_LEAD:total, c0:]
                                     + src[POOL_LEAD - shift:total - shift, c0:])
    cur = e0[base:total, :]
    cg = cg_ref[...]
    for gi in range(len(POOL_WINDOWS)):
        sl = slice(gi * gw, (gi + 1) * gw)
        win = levels[gi + 1][base:total, sl]
        pooled = win / cnt_ref[:, gi:gi + 1] - cur[:, sl]
        mixed = _dot(pooled.astype(BF16), wp_ref[0, gi]) * ps_ref[:, sl]
        o_ref[:, sl] = (mixed[:tq] * _silu(cg[:, sl])).astype(o_ref.dtype)
    e0[POOL_LEAD:base, :] = e0[POOL_LEAD + t_pad:base + t_pad, :]


def _pool_branch(z_main, prev, cnt, wp_bf, pool_scale, *, layer_slot, n_batch, seq, tq, out_dtype):
    rows = z_main.shape[0]
    w = wp_bf.shape[1] * wp_bf.shape[2]
    t_pad = max(tq, POOL_PREV)
    n_t = seq // tq
    ext = pltpu.VMEM((POOL_LEAD + POOL_PREV + t_pad, w), F32)
    return pl.pallas_call(
        functools.partial(_pool_kernel, tq=tq, t_pad=t_pad),
        grid=(n_batch, n_t),
        in_specs=[
            pl.BlockSpec((tq, w), lambda b, t: (b * n_t + t, 0)),
            pl.BlockSpec((tq, w), lambda b, t: (b * n_t + t, 1)),
            pl.BlockSpec((1, POOL_PREV, w), lambda b, t: (b, 0, 0)),
            pl.BlockSpec((t_pad, len(POOL_WINDOWS)), lambda b, t: (t, 0)),
            pl.BlockSpec((1,) + wp_bf.shape[1:], lambda b, t: (layer_slot, 0, 0, 0)),
            pl.BlockSpec((1, w), lambda b, t: (0, 0)),
        ],
        out_specs=pl.BlockSpec((tq, w), lambda b, t: (b * n_t + t, 0)),
        out_shape=jax.ShapeDtypeStruct((rows, w), out_dtype),
        scratch_shapes=[ext] * 5,
        compiler_params=_cparams("arbitrary", "arbitrary"),
        name="pool_branch",
    )(z_main, z_main, prev, cnt, wp_bf, pool_scale.reshape(1, w))


def _pool_counts(pos0, n_rows):
    pos = pos0 + np.arange(n_rows)
    return jnp.asarray(np.stack([np.minimum(pos + 1, wd) for wd in POOL_WINDOWS], axis=1).astype(np.float32))


def _split3(x):
    a = x.astype(BF16)
    r1 = x - a.astype(F32)
    b = r1.astype(BF16)
    c = (r1 - b.astype(F32)).astype(BF16)
    return a, b, c


def _fcum_kernel(lf_ref, fc_ref, fr_ref, *, blk):
    n_blk = lf_ref.shape[0] // blk
    tri = (lax.broadcasted_iota(jnp.int32, (blk, blk), 0)
           >= lax.broadcasted_iota(jnp.int32, (blk, blk), 1)).astype(BF16)
    carry = jnp.zeros((1, LANES), F32)
    for i in range(n_blk):
        a, b, c = _split3(lf_ref[i * blk:(i + 1) * blk, :])
        cs = (_dot(tri, a) + _dot(tri, b)) + _dot(tri, c) + carry
        carry = cs[blk - 1:blk, :]
        fc_ref[i * blk:(i + 1) * blk, :] = cs
        fr_ref[0, i] = cs.T[:D_HEADS, :]


def _fcum(lf, *, n_batch, seq, blk):
    return pl.pallas_call(
        functools.partial(_fcum_kernel, blk=blk),
        grid=(n_batch,),
        in_specs=[pl.BlockSpec((seq, LANES), lambda b: (b, 0))],
        out_specs=[pl.BlockSpec((seq, LANES), lambda b: (b, 0)),
                   pl.BlockSpec((1, seq // blk, D_HEADS, blk), lambda b: (b, 0, 0, 0))],
        out_shape=[jax.ShapeDtypeStruct((n_batch * seq, LANES), F32),
                   jax.ShapeDtypeStruct((n_batch, seq // blk, D_HEADS, blk), F32)],
        compiler_params=_cparams("arbitrary"),
        name="forget_cumsum",
    )(lf)


SCORE_LEAD = 2


def _attn_prompt_kernel(q_ref, dg_ref, k_ref, v_ref, fc_ref, fr_ref, o_ref,
                        kb_scr, vt_scr, qt_scr, *acc_scrs, tq):
    i = pl.program_id(1)
    dh = LANES
    scale = dh ** -0.5
    n_blk = k_ref.shape[1] // tq

    @pl.when(i == 0)
    def _():
        kb_scr[...] = k_ref[0].astype(BF16)
        for h in range(D_HEADS):
            for kb in range(n_blk):
                vt_scr[kb, h * dh:(h + 1) * dh, :] = (
                    v_ref[0, kb * tq:(kb + 1) * tq, h * dh:(h + 1) * dh].T.astype(BF16))

    for h in range(D_HEADS):
        qt_scr[h] = q_ref[:, h * dh:(h + 1) * dh].T.astype(BF16)
        acc_scrs[h][...] = jnp.zeros((dh, tq), F32)
    key_le_query = (lax.broadcasted_iota(jnp.int32, (tq, tq), 0)
                    <= lax.broadcasted_iota(jnp.int32, (tq, tq), 1))

    def block(kb, stats, masked):
        r0 = pl.multiple_of(kb * tq, tq)
        new_stats = []

        def scores(h):
            sl = slice(h * dh, (h + 1) * dh)
            return _dot(kb_scr[pl.ds(r0, tq), sl], qt_scr[h]) * scale - fc_ref[pl.ds(r0, tq), h:h + 1]

        pending = [scores(h) for h in range(SCORE_LEAD)]
        for h in range(D_HEADS):
            sl = slice(h * dh, (h + 1) * dh)
            m_old, l_old = stats[h]
            fq = fr_ref[0, 0, h:h + 1, :]
            u = pending.pop(0)
            if h + SCORE_LEAD < D_HEADS:
                pending.append(scores(h + SCORE_LEAD))
            if masked:
                u = jnp.where(key_le_query, u, NEG_BIG)
            m_new = jnp.maximum(m_old, jnp.max(u, axis=0, keepdims=True) + fq)
            alpha = jnp.exp(m_old - m_new)
            p = jnp.exp(u - (m_new - fq))
            new_stats.append((m_new, alpha * l_old + jnp.sum(p, axis=0, keepdims=True)))
            acc_scrs[h][...] = alpha * acc_scrs[h][...] + _dot(vt_scr[kb, sl, :], p.astype(BF16))
        return tuple(new_stats)

    stats0 = tuple((jnp.full((1, tq), NEG_BIG, F32), jnp.zeros((1, tq), F32)) for _ in range(D_HEADS))
    stats = lax.fori_loop(0, i, lambda kb, st: block(kb, st, False), stats0)
    stats = block(i, stats, True)
    for h in range(D_HEADS):
        sl = slice(h * dh, (h + 1) * dh)
        o_ref[:, sl] = ((acc_scrs[h][...] / stats[h][1]).T * _silu(dg_ref[:, sl])).astype(o_ref.dtype)


def _attn_prompt(z_main, k_all, v_all, fcol, frow, *, layer_slot, n_batch, seq, tq, out_dtype):
    rows = z_main.shape[0]
    w = k_all.shape[2]
    n_q = seq // tq
    return pl.pallas_call(
        functools.partial(_attn_prompt_kernel, tq=tq),
        grid=(n_batch, n_q),
        in_specs=[
            pl.BlockSpec((tq, w), lambda b, i: (b * n_q + i, 2)),
            pl.BlockSpec((tq, w), lambda b, i: (b * n_q + i, 3)),
            pl.BlockSpec((1, seq, w), lambda b, i: (layer_slot, b, 0)),
            pl.BlockSpec((1, seq, w), lambda b, i: (layer_slot, b, 0)),
            pl.BlockSpec((seq, LANES), lambda b, i: (b, 0)),
            pl.BlockSpec((1, 1, D_HEADS, tq), lambda b, i: (b, i, 0, 0)),
        ],
        out_specs=pl.BlockSpec((tq, w), lambda b, i: (b * n_q + i, 0)),
        out_shape=jax.ShapeDtypeStruct((rows, w), out_dtype),
        scratch_shapes=[pltpu.VMEM((seq, w), BF16), pltpu.VMEM((n_q, w, tq), BF16),
                        pltpu.VMEM((D_HEADS, LANES, tq), BF16)]
        + [pltpu.VMEM((LANES, tq), F32)] * D_HEADS,
        compiler_params=_cparams("arbitrary", "arbitrary"),
        name="attn_prompt",
    )(z_main, z_main, k_all, v_all, fcol, frow)


PAGES_PER_STEP = 8


def _attn_sample_tables(n_new, g_pages):
    pg = np.arange(LANES)
    same_head = (pg[:, None] % D_HEADS) == (pg[None, :] % D_HEADS)
    later = (pg[:, None] // D_HEADS) > (pg[None, :] // D_HEADS)
    c_later = (same_head & later).astype(np.float32)
    c_same = same_head.astype(np.float32)
    rows = np.arange(g_pages * SUBLANES)
    r_later = (rows[None, :] > rows[:, None]).astype(np.float32)
    r = np.arange(D_HEADS * n_new)
    past_ok = (pg[None, :] % D_HEADS) == (r[:, None] // n_new)
    past_bias = np.where(past_ok, 0.0, NEG_BIG).astype(np.float32)
    cn = np.arange(n_new * D_HEADS)
    new_ok = ((cn[None, :] % D_HEADS) == (r[:, None] // n_new)) & ((cn[None, :] // D_HEADS) <= (r[:, None] % n_new))
    new_bias = np.where(new_ok, 0.0, NEG_BIG).astype(np.float32)
    m_cols = ((cn[:, None] % D_HEADS == cn[None, :] % D_HEADS)
              & (cn[:, None] // D_HEADS <= cn[None, :] // D_HEADS)).astype(np.float32)
    m_rows = ((cn[None, :] % D_HEADS == r[:, None] // n_new)
              & (cn[None, :] // D_HEADS <= r[:, None] % n_new)).astype(np.float32)
    return c_later, c_same, r_later, past_bias, new_bias, m_cols, m_rows


def _page_copies(pt_ref, ck_hbm, cv_hbm, clf_hbm, kbuf, vbuf, lfbuf, sems, step, slot, *,
                 layer_slot, n_groups, g_pages):
    b = step // n_groups
    first_page = (n_groups - 1 - step % n_groups) * g_pages
    copies = []
    for g in range(g_pages):
        page = pt_ref[b, first_page + g]
        copies.append(pltpu.make_async_copy(ck_hbm.at[layer_slot, page], kbuf.at[slot, g], sems.at[0, slot]))
        copies.append(pltpu.make_async_copy(cv_hbm.at[layer_slot, page], vbuf.at[slot, g], sems.at[1, slot]))
        copies.append(pltpu.make_async_copy(clf_hbm.at[layer_slot, page], lfbuf.at[slot, g], sems.at[2, slot]))
    return copies


def _attn_sample_kernel(pt_ref, q_ref, kn_ref, vn_ref, lfr_ref, lfc_ref, dg_ref,
                        cl_ref, cs_ref, rl_ref, pb_ref, nb_ref, mc_ref, mr_ref,
                        ck_hbm, cv_hbm, clf_hbm, o_ref,
                        kbuf, vbuf, lfbuf, sems, m_scr, l_scr, acc_scr, run_scr, base_scr, *,
                        layer_slot, n_groups, g_pages):
    step = pl.program_id(0)
    n_steps = pl.num_programs(0)
    slot = step % 2
    grp = step % n_groups
    scale = LANES ** -0.5
    hi = lax.Precision.HIGHEST
    copies = functools.partial(_page_copies, pt_ref, ck_hbm, cv_hbm, clf_hbm, kbuf, vbuf, lfbuf, sems,
                               layer_slot=layer_slot, n_groups=n_groups, g_pages=g_pages)

    @pl.when(step == 0)
    def _():
        for c in copies(0, 0):
            c.start()

    @pl.when(step + 1 < n_steps)
    def _():
        for c in copies(step + 1, 1 - slot):
            c.start()

    qb = q_ref[0].astype(BF16)
    rowc = jnp.sum(mr_ref[...] * lfr_ref[0], axis=-1, keepdims=True)

    @pl.when(grp == 0)
    def _():
        m_scr[...] = jnp.full(m_scr.shape, NEG_BIG, F32)
        l_scr[...] = jnp.zeros(l_scr.shape, F32)
        acc_scr[...] = jnp.zeros(acc_scr.shape, F32)
        run_scr[...] = jnp.zeros(run_scr.shape, F32)
        base_scr[...] = rowc + pb_ref[...]

    for c in copies(step, slot):
        c.wait()

    n_rows = g_pages * SUBLANES
    lf = lfbuf[slot].reshape(n_rows, LANES)
    within = jnp.dot(lf, cl_ref[...], precision=hi, preferred_element_type=F32)
    rowtot = jnp.dot(lf, cs_ref[...], precision=hi, preferred_element_type=F32)
    later_rows = jnp.dot(rl_ref[...], rowtot, precision=hi, preferred_element_type=F32)
    g_past = (within + later_rows) + run_scr[...]
    run_scr[...] = run_scr[...] + jnp.sum(rowtot, axis=0, keepdims=True)

    n_keys = g_pages * PAGE_SIZE * D_HEADS
    s_all = _dot_nt(qb, kbuf[slot].reshape(n_keys, LANES).astype(BF16)) * scale
    base = base_scr[...]
    blocks = []
    s_max = None
    for j in range(n_rows):
        sj = (s_all[:, j * LANES:(j + 1) * LANES] + base) + g_past[j:j + 1, :]
        blocks.append(sj)
        s_max = sj if s_max is None else jnp.maximum(s_max, sj)
    m_old = m_scr[...]
    m_new = jnp.maximum(m_old, jnp.max(s_max, axis=-1, keepdims=True))
    alpha = jnp.exp(m_old - m_new)
    p_sum = None
    p_blocks = []
    for sj in blocks:
        pj = jnp.exp(sj - m_new)
        p_sum = pj if p_sum is None else p_sum + pj
        p_blocks.append(pj.astype(BF16))
    l_scr[...] = alpha * l_scr[...] + jnp.sum(p_sum, axis=-1, keepdims=True)
    acc_scr[...] = alpha * acc_scr[...] + _dot(jnp.concatenate(p_blocks, axis=1),
                                               vbuf[slot].reshape(n_keys, LANES).astype(BF16))
    m_scr[...] = m_new

    @pl.when(grp == n_groups - 1)
    def _():
        c_new = jnp.sum(mc_ref[...] * lfc_ref[0], axis=0, keepdims=True)
        s = (_dot_nt(qb, kn_ref[0].astype(BF16)) * scale + rowc) - c_new + nb_ref[...]
        m_old = m_scr[...]
        m_new = jnp.maximum(m_old, jnp.max(s, axis=-1, keepdims=True))
        alpha = jnp.exp(m_old - m_new)
        pr = jnp.exp(s - m_new)
        l_fin = alpha * l_scr[...] + jnp.sum(pr, axis=-1, keepdims=True)
        acc = alpha * acc_scr[...] + _dot(pr.astype(BF16), vn_ref[0].astype(BF16))
        o_ref[0] = (acc / l_fin) * _silu(dg_ref[0])


def _attn_sample(q_hq, kn, vn, lf_row, lf_col, dg_hq, cache_k, cache_v, cache_lf, page_table, *, layer_slot):
    bd, r, dh = q_hq.shape
    n_pages = page_table.shape[1]
    n_new = r // D_HEADS
    n_slots, n_phys = cache_k.shape[:2]
    pg_rows = PAGE_SIZE * D_HEADS
    ck = cache_k.reshape(n_slots, n_phys, pg_rows, dh)
    cv = cache_v.reshape(n_slots, n_phys, pg_rows, dh)
    clf = cache_lf.reshape(n_slots, n_phys, SUBLANES, LANES)
    g_pages = min(PAGES_PER_STEP, n_pages)
    assert n_pages % g_pages == 0
    n_groups = n_pages // g_pages
    tabs = [jnp.asarray(t) for t in _attn_sample_tables(n_new, g_pages)]

    def per_b(shape):
        return pl.BlockSpec((1,) + shape, lambda s, pt: (s // n_groups, 0, 0))

    def const(t):
        return pl.BlockSpec(t.shape, lambda s, pt: (0, 0))

    hbm = pl.BlockSpec(memory_space=pl.ANY)
    grid_spec = pltpu.PrefetchScalarGridSpec(
        num_scalar_prefetch=1,
        grid=(bd * n_groups,),
        in_specs=[per_b((r, dh)), per_b((r, dh)), per_b((r, dh)), per_b((1, r)), per_b((r, 1)), per_b((r, dh))]
        + [const(t) for t in tabs] + [hbm, hbm, hbm],
        out_specs=per_b((r, dh)),
        scratch_shapes=[
            pltpu.VMEM((2, g_pages, pg_rows, dh), F32), pltpu.VMEM((2, g_pages, pg_rows, dh), F32),
            pltpu.VMEM((2, g_pages, SUBLANES, LANES), F32), pltpu.SemaphoreType.DMA((3, 2)),
            pltpu.VMEM((r, 1), F32), pltpu.VMEM((r, 1), F32), pltpu.VMEM((r, dh), F32),
            pltpu.VMEM((1, LANES), F32), pltpu.VMEM((r, LANES), F32)],
    )
    return pl.pallas_call(
        functools.partial(_attn_sample_kernel, layer_slot=layer_slot, n_groups=n_groups, g_pages=g_pages),
        grid_spec=grid_spec,
        out_shape=jax.ShapeDtypeStruct((bd, r, dh), F32),
        compiler_params=_cparams("arbitrary"),
        name="attn_sample",
    )(page_table, q_hq, kn, vn, lf_row, lf_col, dg_hq, *tabs, ck, cv, clf)


TM_NORM = 1024
TM_IN = 2048
TN_IN_EVEN = 512
TN_IN_ODD = 256
TN_IN_SAMPLE = 1024
TM_OUT = 512
TQ_POOL = 256
TQ_ATTN = 256


def kernel(x_prompt, x_sample, c_prompt, c_sample, state_ret, state_pool, cache_k, cache_v, cache_logf,
           page_table, g_pre, g_post, w_ada, b_ada, w_in_even, w_out_even, ln_a_g, ln_a_b, w_s, b_s,
           ret_g, w_in_odd, b_f, w_out_odd, w_pool, pool_scale):
    bp, seq, d = x_prompt.shape
    bd, n_new, _ = x_sample.shape
    depth = g_pre.shape[0]
    n_odd = w_in_odd.shape[0]
    w = d // 2
    n_past = page_table.shape[1] * PAGE_SIZE
    rows_s = bd * n_new

    c_all = jnp.concatenate([c_prompt, c_sample], axis=0)
    c_all = jnp.pad(c_all, ((0, -c_all.shape[0] % (2 * SUBLANES)), (0, 0)))
    mod = _ada_mod(c_all, w_ada, b_ada)

    def mods(l):
        mp = mod[l, :bp].reshape(bp, 1, 3 * d)
        ms = jnp.repeat(mod[l, bp:bp + bd], n_new, axis=0).reshape(1, rows_s, 3 * d)
        return [(m[..., :d], m[..., d:2 * d], m[..., 2 * d:]) for m in (mp, ms)]

    xp = x_prompt.reshape(bp * seq, d)
    xs = x_sample.reshape(rows_s, d)
    ret_p, ret_s, gv_s, pool_p, pool_s, lfp_l, lfs_l = [], [], [], [], [], [], []
    kv_p = kv_s = None
    zeros_state = jnp.zeros((bp,) + state_ret.shape[2:], F32)
    w_out_even_bf, w_out_odd_bf, w_pool_bf = w_out_even.astype(BF16), w_out_odd.astype(BF16), w_pool.astype(BF16)
    all_mods = [mods(l) for l in range(depth)]
    (shift_p, scale_p, _), (shift_s, scale_s, _) = all_mods[0]
    g0 = g_pre[0].reshape(1, d)
    hp = _modnorm_call(xp, scale_p, shift_p, g0, per_row=False, rows_per_mod=seq, tm=TM_NORM)
    hs = _modnorm_call(xs, scale_s, shift_s, g0, per_row=True, rows_per_mod=1, tm=rows_s)
    for l in range(depth):
        j = l // 2
        (_, _, gate_p), (_, _, gate_s) = all_mods[l]
        gpost = g_post[l].reshape(1, d)
        if l % 2 == 0:
            w_out = w_out_even_bf
            zp = _inproj_even(hp, w_in_even, layer_slot=j, tm=TM_IN, tn=TN_IN_EVEN)
            zs = _inproj_even(hs, w_in_even, layer_slot=j, tm=rows_s, tn=TN_IN_SAMPLE)
            mix = functools.partial(_even_mixer, ln_g=ln_a_g[j], ln_b=ln_a_b[j], w_s=w_s[j], b_s=b_s[j],
                                    ret_g=ret_g[j])
            ap, rp, sp = mix(zp, zeros_state, n_batch=bp, seq=seq, pos0=0, emit_vn=False, out_dtype=BF16)
            as_, rs, ss, vn_s = mix(zs, state_ret[j], n_batch=bd, seq=n_new, pos0=n_past, emit_vn=True,
                                    out_dtype=F32)
            ret_p.append(sp)
            ret_s.append(ss)
            gv_s.append(vn_s.reshape(bd, n_new, w))
        else:
            wf = jnp.pad(w_in_odd[j, :, 6 * w:], ((0, 0), (0, LANES - D_HEADS))).astype(BF16)
            bf = jnp.pad(b_f[j], (0, LANES - D_HEADS)).reshape(1, LANES)
            w_out = w_out_odd_bf
            wp = w_pool_bf
            zp, kp_all, vp_all, lfp = _inproj_odd(hp, w_in_odd, wf, bf, kv_p, layer_slot=j, n_slots=n_odd,
                                                  tm=TM_IN, tn=TN_IN_ODD)
            kv_p = (kp_all, vp_all)
            zs, ks_all, vs_all, lfs = _inproj_odd(hs, w_in_odd, wf, bf, kv_s, layer_slot=j, n_slots=n_odd,
                                                  tm=rows_s, tn=TN_IN_SAMPLE)
            kv_s = (ks_all, vs_all)
            ap = _pool_branch(zp, jnp.zeros((bp, POOL_PREV, w), F32), _pool_counts(0, seq), wp, pool_scale[j],
                              layer_slot=j, n_batch=bp, seq=seq, tq=TQ_POOL, out_dtype=BF16)
            fcol, frow = _fcum(lfp, n_batch=bp, seq=seq, blk=TQ_ATTN)
            rp = _attn_prompt(zp, kp_all, vp_all, fcol, frow, layer_slot=j, n_batch=bp, seq=seq, tq=TQ_ATTN,
                              out_dtype=BF16)
            pool_p.append(zp.reshape(bp, seq, 4 * w)[:, seq - POOL_BUF:, :w])
            lfp_l.append(lfp[:, :D_HEADS].reshape(bp, seq, D_HEADS))
            prev = jnp.pad(state_pool[j], ((0, 0), (POOL_PREV - POOL_BUF, 0), (0, 0)))
            as_ = _pool_branch(zs, prev, _pool_counts(n_past, POOL_PREV), wp, pool_scale[j],
                               layer_slot=j, n_batch=bd, seq=n_new, tq=n_new, out_dtype=F32)
            zs4 = zs.reshape(bd, n_new, 4, D_HEADS, LANES)
            to_hq = lambda a: a.transpose(0, 2, 1, 3).reshape(bd, D_HEADS * n_new, LANES)
            lf_new = lfs[:, :D_HEADS].reshape(bd, n_new * D_HEADS)
            o_hq = _attn_sample(to_hq(zs4[:, :, 2]), ks_all[j].reshape(bd, n_new * D_HEADS, LANES),
                                vs_all[j].reshape(bd, n_new * D_HEADS, LANES),
                                lf_new.reshape(bd, 1, -1), lf_new.reshape(bd, -1, 1), to_hq(zs4[:, :, 3]),
                                cache_k, cache_v, cache_logf, page_table, layer_slot=j)
            rs = o_hq.reshape(bd, D_HEADS, n_new, LANES).transpose(0, 2, 1, 3).reshape(rows_s, w)
            c_in_s = zs[:, :w].reshape(bd, n_new, w)
            pool_s.append(jnp.concatenate([state_pool[j], c_in_s], axis=1)[:, -POOL_BUF:])
            lfs_l.append(lfs[:, :D_HEADS].reshape(bd, n_new, D_HEADS))
        next_p = next_s = None
        if l + 1 < depth:
            (shift_p, scale_p, _), (shift_s, scale_s, _) = all_mods[l + 1]
            g_next = g_pre[l + 1].reshape(1, d)
            next_p, next_s = (scale_p, shift_p, g_next), (scale_s, shift_s, g_next)
        xp, hp = _outproj(ap, rp, xp, gate_p, gpost, w_out, next_p, layer_slot=j, per_row=False,
                          rows_per_mod=seq, tm=TM_OUT)
        xs, hs = _outproj(as_, rs, xs, gate_s, gpost, w_out, next_s, layer_slot=j, per_row=True,
                          rows_per_mod=1, tm=rows_s)

    dh = LANES
    return (xp.reshape(bp, seq, d), xs.reshape(bd, n_new, d),
            jnp.stack(ret_p), jnp.stack(ret_s), jnp.stack(gv_s), jnp.stack(pool_p), jnp.stack(pool_s),
            kv_p[0].reshape(n_odd, bp, seq, D_HEADS, dh), kv_p[1].reshape(n_odd, bp, seq, D_HEADS, dh),
            jnp.stack(lfp_l),
            kv_s[0].reshape(n_odd, bd, n_new, D_HEADS, dh), kv_s[1].reshape(n_odd, bd, n_new, D_HEADS, dh),
            jnp.stack(lfs_l))
```

```python
import functools

import numpy as np
import jax
import jax.numpy as jnp
from jax import lax
from jax.experimental import pallas as pl
from jax.experimental.pallas import tpu as pltpu

F32 = jnp.float32
BF16 = jnp.bfloat16

EPS = 1e-6
ROPE_BASE = 10000.0
CHUNK = 128
A_GROUPS = 8
B_HEADS = 4
POOL_WINDOWS = (2, 4, 8, 16)
POOL_BUF = 15
D_HEADS = 8
PAGE_SIZE = 128
LANES = 128
SUBLANES = 8
NEG_BIG = -1e30
VMEM_LIMIT = 56 * 1024 * 1024


def _cparams(*sem):
    return pltpu.CompilerParams(dimension_semantics=sem, vmem_limit_bytes=VMEM_LIMIT)


def _silu(x):
    return x * jax.nn.sigmoid(x)


def _log_sigmoid(x):
    return jnp.minimum(x, 0.0) - jnp.log1p(jnp.exp(-jnp.abs(x)))


def _dot(a, b):
    return jnp.dot(a, b, preferred_element_type=F32)


def _dot_nt(a, b):
    return lax.dot_general(a, b, (((1,), (1,)), ((), ())), preferred_element_type=F32)


def _dot_tn(a, b):
    return lax.dot_general(a, b, (((0,), (0,)), ((), ())), preferred_element_type=F32)


def _ada_kernel(c_ref, w_ref, b_ref, o_ref):
    a = _silu(c_ref[...]).astype(BF16)
    o_ref[0] = _dot(a, w_ref[0].astype(BF16)) + b_ref[0]


def _ada_mod(c_all, w_ada, b_ada, tn=1024):
    depth, d, n = w_ada.shape
    r = c_all.shape[0]
    return pl.pallas_call(
        _ada_kernel,
        grid=(depth, n // tn),
        in_specs=[
            pl.BlockSpec((r, d), lambda l, j: (0, 0)),
            pl.BlockSpec((1, d, tn), lambda l, j: (l, 0, j)),
            pl.BlockSpec((1, 1, tn), lambda l, j: (l, 0, j)),
        ],
        out_specs=pl.BlockSpec((1, r, tn), lambda l, j: (l, 0, j)),
        out_shape=jax.ShapeDtypeStruct((depth, r, n), F32),
        compiler_params=_cparams("arbitrary", "arbitrary"),
        name="ada_mod",
    )(c_all, w_ada, b_ada.reshape(depth, 1, n))


def _modnorm(x, scale, shift, g):
    y = x * lax.rsqrt(jnp.mean(x * x, axis=-1, keepdims=True) + EPS) * g
    return (y * (1.0 + scale) + shift).astype(BF16)


def _modnorm_kernel(x_ref, sc_ref, sh_ref, g_ref, h_ref):
    h_ref[...] = _modnorm(x_ref[...], sc_ref[0], sh_ref[0], g_ref[...])


def _mod_spec(per_row, rows_per_mod, tm, d):
    if per_row:
        return pl.BlockSpec((1, tm, d), lambda i: (0, i, 0))
    return pl.BlockSpec((1, 1, d), lambda i: ((i * tm) // rows_per_mod, 0, 0))


def _modnorm_call(x2d, scale, shift, g, *, per_row, rows_per_mod, tm):
    rows, d = x2d.shape
    return pl.pallas_call(
        _modnorm_kernel,
        grid=(rows // tm,),
        in_specs=[pl.BlockSpec((tm, d), lambda i: (i, 0)),
                  _mod_spec(per_row, rows_per_mod, tm, d), _mod_spec(per_row, rows_per_mod, tm, d),
                  pl.BlockSpec((1, d), lambda i: (0, 0))],
        out_specs=pl.BlockSpec((tm, d), lambda i: (i, 0)),
        out_shape=jax.ShapeDtypeStruct((rows, d), BF16),
        compiler_params=_cparams("arbitrary"),
        name="modnorm",
    )(x2d, scale, shift, g)


def _inproj_even_kernel(h_ref, w_ref, z_ref):
    z_ref[...] = _dot(h_ref[...], w_ref[0].astype(BF16))


def _inproj_odd_kernel(h_ref, w_ref, wf_ref, bf_ref, *rest, tiles_per_group, kv_slot):
    z_ref, k_ref, v_ref, lf_ref = rest[-4:]
    j = pl.program_id(1)

    def project():
        return _dot_nt(h_ref[...], w_ref[0].astype(BF16))

    def put(kv_ref):
        for s in range(kv_ref.shape[0]):
            kv_ref[s] = project() if s == kv_slot else jnp.zeros(kv_ref.shape[1:], F32)

    @pl.when(j == 0)
    def _():
        lf_ref[...] = _log_sigmoid(_dot_nt(h_ref[...], wf_ref[...].astype(BF16)) + bf_ref[...])

    group = j // tiles_per_group

    @pl.when(jnp.logical_or(group < 3, group == 5))
    def _():
        z_ref[...] = project()

    @pl.when(group == 3)
    def _():
        put(k_ref)

    @pl.when(group == 4)
    def _():
        put(v_ref)


def _inproj_even(h, w_all, *, layer_slot, tm, tn):
    rows, d = h.shape
    n = w_all.shape[2]
    return pl.pallas_call(
        _inproj_even_kernel,
        grid=(rows // tm, n // tn),
        in_specs=[
            pl.BlockSpec((tm, d), lambda i, j: (i, 0)),
            pl.BlockSpec((1, d, tn), lambda i, j: (layer_slot, 0, j)),
        ],
        out_specs=pl.BlockSpec((tm, tn), lambda i, j: (i, j)),
        out_shape=jax.ShapeDtypeStruct((rows, n), F32),
        compiler_params=_cparams("arbitrary", "arbitrary"),
        name="inproj_even",
    )(h, w_all)


def _inproj_odd(h, w_all_t, wf, bf, kv_prev, *, layer_slot, n_slots, tm, tn):
    rows, d = h.shape
    w = (w_all_t.shape[1] - D_HEADS) // 6
    tpg = w // tn
    n_tiles = 6 * tpg

    creates_kv = kv_prev is None
    kv_block = (n_slots if creates_kv else 1, tm, tn)
    kv_first = 0 if creates_kv else layer_slot

    def zmap(i, j):
        return (i, j - jnp.clip(j - (3 * tpg - 1), 0, 2 * tpg))

    def kmap(i, j):
        return (kv_first, i, jnp.clip(j - 3 * tpg, 0, tpg - 1))

    def vmap(i, j):
        return (kv_first, i, jnp.clip(j - 4 * tpg, 0, tpg - 1))

    in_specs = [
        pl.BlockSpec((tm, d), lambda i, j: (i, 0)),
        pl.BlockSpec((1, tn, d), lambda i, j: (layer_slot, j, 0)),
        pl.BlockSpec((LANES, d), lambda i, j: (0, 0)),
        pl.BlockSpec((1, LANES), lambda i, j: (0, 0)),
    ]
    args = [h, w_all_t, wf, bf]
    aliases = {}
    if kv_prev is not None:
        in_specs += [pl.BlockSpec(memory_space=pl.ANY)] * 2
        aliases = {len(args): 1, len(args) + 1: 2}
        args += list(kv_prev)
    kv_shape = jax.ShapeDtypeStruct((n_slots, rows, w), F32)
    return pl.pallas_call(
        functools.partial(_inproj_odd_kernel, tiles_per_group=tpg, kv_slot=layer_slot if creates_kv else 0),
        grid=(rows // tm, n_tiles),
        in_specs=in_specs,
        out_specs=[
            pl.BlockSpec((tm, tn), zmap),
            pl.BlockSpec(kv_block, kmap),
            pl.BlockSpec(kv_block, vmap),
            pl.BlockSpec((tm, LANES), lambda i, j: (i, 0)),
        ],
        out_shape=[
            jax.ShapeDtypeStruct((rows, 4 * w), F32),
            kv_shape,
            kv_shape,
            jax.ShapeDtypeStruct((rows, LANES), F32),
        ],
        input_output_aliases=aliases,
        compiler_params=_cparams("arbitrary", "arbitrary"),
        name="inproj_odd",
    )(*args)


def _outproj_kernel(a_ref, b_ref, x_ref, gate_ref, g_ref, w1_ref, w2_ref, *rest, with_next):
    y = _dot(a_ref[...].astype(BF16), w1_ref[0]) + _dot(b_ref[...].astype(BF16), w2_ref[0])
    yn = y * lax.rsqrt(jnp.mean(y * y, axis=-1, keepdims=True) + EPS) * g_ref[...]
    x_new = x_ref[...] + gate_ref[0] * yn
    if with_next:
        sc_ref, sh_ref, gn_ref, o_ref, h_ref = rest
        h_ref[...] = _modnorm(x_new, sc_ref[0], sh_ref[0], gn_ref[...])
    else:
        o_ref, = rest
    o_ref[...] = x_new


def _outproj(a, b, x2d, gate, g, w_bf, next_mod, *, layer_slot, per_row, rows_per_mod, tm):
    rows, d = x2d.shape
    w = a.shape[1]
    mod_spec = _mod_spec(per_row, rows_per_mod, tm, d)
    row_spec = pl.BlockSpec((tm, d), lambda i: (i, 0))
    vec_spec = pl.BlockSpec((1, d), lambda i: (0, 0))
    in_specs = [
        pl.BlockSpec((tm, w), lambda i: (i, 0)),
        pl.BlockSpec((tm, w), lambda i: (i, 0)),
        row_spec, mod_spec, vec_spec,
        pl.BlockSpec((1, w, d), lambda i: (layer_slot, 0, 0)),
        pl.BlockSpec((1, w, d), lambda i: (layer_slot, 1, 0)),
    ]
    args = [a, b, x2d, gate, g, w_bf, w_bf]
    out_specs = [row_spec]
    out_shape = [jax.ShapeDtypeStruct((rows, d), F32)]
    if next_mod is not None:
        in_specs += [mod_spec, mod_spec, vec_spec]
        args += list(next_mod)
        out_specs.append(row_spec)
        out_shape.append(jax.ShapeDtypeStruct((rows, d), BF16))
    res = pl.pallas_call(
        functools.partial(_outproj_kernel, with_next=next_mod is not None),
        grid=(rows // tm,),
        in_specs=in_specs,
        out_specs=out_specs,
        out_shape=out_shape,
        compiler_params=_cparams("arbitrary"),
        name="outproj",
    )(*args)
    return res if next_mod is not None else (res[0], None)


def _retention_tables(c_len, pos0, n_rows):
    lg = np.log(1.0 - 2.0 ** (-5.0 - np.arange(B_HEADS, dtype=np.float64)))
    t = np.arange(CHUNK, dtype=np.float64)
    diff = t[:, None] - t[None, :]
    dmask = np.where(diff >= 0, np.exp(lg[:, None, None] * np.maximum(diff, 0.0)), 0.0).astype(np.float32)
    qdec = np.exp(lg[None, :] * (t + 1.0)[:, None]).astype(np.float32)
    kdec = np.exp(lg[None, :] * np.maximum(c_len - 1.0 - t, 0.0)[:, None]).astype(np.float32)
    cdec = tuple(float(v) for v in np.exp(lg * c_len).astype(np.float32))
    half = LANES
    inv = ROPE_BASE ** (-np.arange(half, dtype=np.float64) / half)
    pos = (pos0 + np.arange(n_rows)).astype(np.float64)
    ang = pos[:, None] * inv[None, :]
    return dmask, qdec, kdec, cdec, ang


def _even_mixer_kernel(u_ref, v_ref, ga_ref, q_ref, k_ref, vv_ref, gr_ref,
                       lng_ref, lnb_ref, ws_ref, bst_ref, retg_ref, cos_ref, sin_ref,
                       dmask_ref, qdec_ref, kdec_ref, s0_ref, *rest, c_in, cdec, emit_vn):
    if emit_vn:
        ao_ref, ro_ref, so_ref, vn_ref, s_scr = rest
    else:
        ao_ref, ro_ref, so_ref, s_scr = rest
    c = pl.program_id(1)

    @pl.when(c == 0)
    def _():
        s_scr[...] = s0_ref[0, 0]

    def ld(ref):
        x = ref[...]
        if c_in == CHUNK:
            return x
        return jnp.concatenate([x, jnp.zeros((CHUNK - c_in, x.shape[1]), F32)], axis=0)

    v = ld(v_ref)
    xc = v - jnp.mean(v, axis=-1, keepdims=True)
    vn = xc * lax.rsqrt(jnp.mean(xc * xc, axis=-1, keepdims=True) + EPS) * lng_ref[...] + lnb_ref[...]
    if emit_vn:
        vn_ref[...] = vn[:c_in]
    vnb = vn.astype(BF16)
    u = ld(u_ref)
    ga = ld(ga_ref)
    causal = (lax.broadcasted_iota(jnp.int32, (CHUNK, CHUNK), 0)
              >= lax.broadcasted_iota(jnp.int32, (CHUNK, CHUNK), 1))
    gw = u.shape[1] // A_GROUPS
    for g in range(A_GROUPS):
        sl = slice(g * gw, (g + 1) * gw)
        wm = jnp.where(causal, ws_ref[g], 0.0).astype(BF16)
        mixed = _dot(wm, vnb[:, sl]) + bst_ref[:, g:g + 1]
        ao_ref[:, sl] = (u[:, sl] * mixed * _silu(ga[:, sl]))[:c_in].astype(ao_ref.dtype)

    cos = cos_ref[...]
    sin = sin_ref[...]
    q = ld(q_ref)
    k = ld(k_ref)
    vv = ld(vv_ref)
    gr = ld(gr_ref)
    dk = q.shape[1] // B_HEADS
    half = dk // 2
    for h in range(B_HEADS):
        sl = slice(h * dk, (h + 1) * dk)
        q1, q2 = q[:, h * dk:h * dk + half], q[:, h * dk + half:(h + 1) * dk]
        k1, k2 = k[:, h * dk:h * dk + half], k[:, h * dk + half:(h + 1) * dk]
        qr = jnp.concatenate([q1 * cos - q2 * sin, q1 * sin + q2 * cos], axis=1)
        kr = jnp.concatenate([k1 * cos - k2 * sin, k1 * sin + k2 * cos], axis=1) * (dk ** -0.5)
        qrb = qr.astype(BF16)
        vb = vv[:, sl].astype(BF16)
        inner = _dot_nt(qrb, kr.astype(BF16)) * dmask_ref[h]
        s = s_scr[h]
        o = _dot(inner.astype(BF16), vb) + _dot(qrb, s.astype(BF16)) * qdec_ref[:, h:h + 1]
        kd = (kr * kdec_ref[:, h:h + 1]).astype(BF16)
        s_scr[h] = s * cdec[h] + _dot_tn(kd, vb)
        on = o * lax.rsqrt(jnp.mean(o * o, axis=-1, keepdims=True) + EPS) * retg_ref[:, sl]
        ro_ref[:, sl] = (on * _silu(gr[:, sl]))[:c_in].astype(ro_ref.dtype)

    @pl.when(c == pl.num_programs(1) - 1)
    def _():
        so_ref[0] = s_scr[...]


def _even_mixer(z, s0_all, ln_g, ln_b, w_s, b_s, ret_g, *, s0_slot, n_batch, seq, pos0, emit_vn, out_dtype):
    rows, n = z.shape
    s_shape = s0_all.shape[2:]
    w = n // 7
    c_in = min(seq, CHUNK)
    n_chunks = seq // c_in
    c_len = CHUNK if seq % CHUNK == 0 else seq
    dmask, qdec, kdec, cdec, ang = _retention_tables(c_len, pos0, n_chunks * CHUNK)
    cos = jnp.asarray(np.cos(ang).astype(np.float32))
    sin = jnp.asarray(np.sin(ang).astype(np.float32))

    def zspec(col):
        return pl.BlockSpec((c_in, w), lambda b, c, col=col: (b * n_chunks + c, col))

    def const(shape):
        return pl.BlockSpec(shape, lambda b, c: (0,) * len(shape))

    row_spec = pl.BlockSpec((c_in, w), lambda b, c: (b * n_chunks + c, 0))
    state_in_spec = pl.BlockSpec((1, 1) + s_shape, lambda b, c: (s0_slot, b, 0, 0, 0))
    state_spec = pl.BlockSpec((1,) + s_shape, lambda b, c: (b, 0, 0, 0))
    out_specs = [row_spec, row_spec, state_spec]
    out_shape = [jax.ShapeDtypeStruct((rows, w), out_dtype),
                 jax.ShapeDtypeStruct((rows, w), out_dtype),
                 jax.ShapeDtypeStruct((n_batch,) + s_shape, F32)]
    if emit_vn:
        out_specs.append(row_spec)
        out_shape.append(jax.ShapeDtypeStruct((rows, w), F32))
    return pl.pallas_call(
        functools.partial(_even_mixer_kernel, c_in=c_in, cdec=cdec, emit_vn=emit_vn),
        grid=(n_batch, n_chunks),
        in_specs=[zspec(i) for i in range(7)] + [
            const((1, w)), const((1, w)), const(w_s.shape), const((CHUNK, A_GROUPS)), const((1, w)),
            pl.BlockSpec((CHUNK, LANES), lambda b, c: (c, 0)),
            pl.BlockSpec((CHUNK, LANES), lambda b, c: (c, 0)),
            const(dmask.shape), const(qdec.shape), const(kdec.shape),
            state_in_spec,
        ],
        out_specs=out_specs,
        out_shape=out_shape,
        scratch_shapes=[pltpu.VMEM(s_shape, F32)],
        compiler_params=_cparams("arbitrary", "arbitrary"),
        name="even_mixer",
    )(z, z, z, z, z, z, z, ln_g.reshape(1, w), ln_b.reshape(1, w), w_s, b_s.T, ret_g.reshape(1, w),
      cos, sin, jnp.asarray(dmask), jnp.asarray(qdec), jnp.asarray(kdec), s0_all)


POOL_PREV = 16
POOL_LEAD = SUBLANES


def _pool_kernel(cin_ref, cg_ref, prev_ref, cnt_ref, wp_ref, ps_ref, o_ref, e0, e1, e2, e3, e4, *, tq, t_pad):
    t = pl.program_id(1)
    base = POOL_LEAD + POOL_PREV
    total = base + t_pad
    w = cin_ref.shape[1]
    gw = w // len(POOL_WINDOWS)

    @pl.when(t == 0)
    def _():
        for e in (e0, e1, e2, e3, e4):
            e[0:POOL_LEAD, :] = jnp.zeros((POOL_LEAD, w), F32)
        e0[POOL_LEAD:base, :] = prev_ref[0]
        if t_pad > tq:
            e0[base + tq:total, :] = jnp.zeros((t_pad - tq, w), F32)

    e0[base:base + tq, :] = cin_ref[...]
    levels = (e0, e1, e2, e3, e4)
    for lvl in range(1, 5):
        shift = 1 << (lvl - 1)
        c0 = (lvl - 1) * gw
        src, dst = levels[lvl - 1], levels[lvl]
        dst[POOL_LEAD:total, c0:] = (src[POOL_LEAD:total, c0:]
                                     + src[POOL_LEAD - shift:total - shift, c0:])
    cur = e0[base:total, :]
    cg = cg_ref[...]
    for gi in range(len(POOL_WINDOWS)):
        sl = slice(gi * gw, (gi + 1) * gw)
        win = levels[gi + 1][base:total, sl]
        pooled = win / cnt_ref[:, gi:gi + 1] - cur[:, sl]
        mixed = _dot(pooled.astype(BF16), wp_ref[0, gi]) * ps_ref[:, sl]
        o_ref[:, sl] = (mixed[:tq] * _silu(cg[:, sl])).astype(o_ref.dtype)
    e0[POOL_LEAD:base, :] = e0[POOL_LEAD + t_pad:base + t_pad, :]


def _pool_branch(z_main, prev, cnt, wp_bf, pool_scale, *, layer_slot, n_batch, seq, tq, out_dtype):
    rows = z_main.shape[0]
    w = wp_bf.shape[1] * wp_bf.shape[2]
    t_pad = max(tq, POOL_PREV)
    n_t = seq // tq
    ext = pltpu.VMEM((POOL_LEAD + POOL_PREV + t_pad, w), F32)
    return pl.pallas_call(
        functools.partial(_pool_kernel, tq=tq, t_pad=t_pad),
        grid=(n_batch, n_t),
        in_specs=[
            pl.BlockSpec((tq, w), lambda b, t: (b * n_t + t, 0)),
            pl.BlockSpec((tq, w), lambda b, t: (b * n_t + t, 1)),
            pl.BlockSpec((1, POOL_PREV, w), lambda b, t: (b, 0, 0)),
            pl.BlockSpec((t_pad, len(POOL_WINDOWS)), lambda b, t: (t, 0)),
            pl.BlockSpec((1,) + wp_bf.shape[1:], lambda b, t: (layer_slot, 0, 0, 0)),
            pl.BlockSpec((1, w), lambda b, t: (0, 0)),
        ],
        out_specs=pl.BlockSpec((tq, w), lambda b, t: (b * n_t + t, 0)),
        out_shape=jax.ShapeDtypeStruct((rows, w), out_dtype),
        scratch_shapes=[ext] * 5,
        compiler_params=_cparams("arbitrary", "arbitrary"),
        name="pool_branch",
    )(z_main, z_main, prev, cnt, wp_bf, pool_scale.reshape(1, w))


def _pool_counts(pos0, n_rows):
    pos = pos0 + np.arange(n_rows)
    return jnp.asarray(np.stack([np.minimum(pos + 1, wd) for wd in POOL_WINDOWS], axis=1).astype(np.float32))


def _split3(x):
    a = x.astype(BF16)
    r1 = x - a.astype(F32)
    b = r1.astype(BF16)
    c = (r1 - b.astype(F32)).astype(BF16)
    return a, b, c


def _fcum_kernel(lf_ref, fc_ref, fr_ref, *, blk):
    n_blk = lf_ref.shape[0] // blk
    tri = (lax.broadcasted_iota(jnp.int32, (blk, blk), 0)
           >= lax.broadcasted_iota(jnp.int32, (blk, blk), 1)).astype(BF16)
    carry = jnp.zeros((1, LANES), F32)
    for i in range(n_blk):
        a, b, c = _split3(lf_ref[i * blk:(i + 1) * blk, :])
        cs = (_dot(tri, a) + _dot(tri, b)) + _dot(tri, c) + carry
        carry = cs[blk - 1:blk, :]
        fc_ref[i * blk:(i + 1) * blk, :] = cs
        fr_ref[0, i] = cs.T[:D_HEADS, :]


def _fcum(lf, *, n_batch, seq, blk):
    return pl.pallas_call(
        functools.partial(_fcum_kernel, blk=blk),
        grid=(n_batch,),
        in_specs=[pl.BlockSpec((seq, LANES), lambda b: (b, 0))],
        out_specs=[pl.BlockSpec((seq, LANES), lambda b: (b, 0)),
                   pl.BlockSpec((1, seq // blk, D_HEADS, blk), lambda b: (b, 0, 0, 0))],
        out_shape=[jax.ShapeDtypeStruct((n_batch * seq, LANES), F32),
                   jax.ShapeDtypeStruct((n_batch, seq // blk, D_HEADS, blk), F32)],
        compiler_params=_cparams("arbitrary"),
        name="forget_cumsum",
    )(lf)


SCORE_LEAD = 3


def _attn_prompt_kernel(q_ref, dg_ref, k_ref, v_ref, fc_ref, fr_ref, o_ref,
                        kb_scr, vt_scr, qt_scr, *acc_scrs, tq):
    i = pl.program_id(1)
    dh = LANES
    scale = dh ** -0.5
    n_blk = k_ref.shape[1] // tq

    @pl.when(i == 0)
    def _():
        kb_scr[...] = k_ref[0].astype(BF16)
        for h in range(D_HEADS):
            for kb in range(n_blk):
                vt_scr[kb, h * dh:(h + 1) * dh, :] = (
                    v_ref[0, kb * tq:(kb + 1) * tq, h * dh:(h + 1) * dh].T.astype(BF16))

    for h in range(D_HEADS):
        qt_scr[h] = q_ref[:, h * dh:(h + 1) * dh].T.astype(BF16)
        acc_scrs[h][...] = jnp.zeros((dh, tq), F32)
    key_le_query = (lax.broadcasted_iota(jnp.int32, (tq, tq), 0)
                    <= lax.broadcasted_iota(jnp.int32, (tq, tq), 1))

    def block(kb, stats, masked):
        r0 = pl.multiple_of(kb * tq, tq)
        new_stats = []

        def scores(h):
            sl = slice(h * dh, (h + 1) * dh)
            return _dot(kb_scr[pl.ds(r0, tq), sl], qt_scr[h]) * scale - fc_ref[pl.ds(r0, tq), h:h + 1]

        pending = [scores(h) for h in range(SCORE_LEAD)]
        for h in range(D_HEADS):
            sl = slice(h * dh, (h + 1) * dh)
            m_old, l_old = stats[h]
            fq = fr_ref[0, 0, h:h + 1, :]
            u = pending.pop(0)
            if h + SCORE_LEAD < D_HEADS:
                pending.append(scores(h + SCORE_LEAD))
            if masked:
                u = jnp.where(key_le_query, u, NEG_BIG)
            m_new = jnp.maximum(m_old, jnp.max(u, axis=0, keepdims=True) + fq)
            alpha = jnp.exp(m_old - m_new)
            p = jnp.exp(u - (m_new - fq))
            new_stats.append((m_new, alpha * l_old + jnp.sum(p, axis=0, keepdims=True)))
            acc_scrs[h][...] = alpha * acc_scrs[h][...] + _dot(vt_scr[kb, sl, :], p.astype(BF16))
        return tuple(new_stats)

    stats0 = tuple((jnp.full((1, tq), NEG_BIG, F32), jnp.zeros((1, tq), F32)) for _ in range(D_HEADS))
    stats = lax.fori_loop(0, i, lambda kb, st: block(kb, st, False), stats0)
    stats = block(i, stats, True)
    for h in range(D_HEADS):
        sl = slice(h * dh, (h + 1) * dh)
        o_ref[:, sl] = ((acc_scrs[h][...] / stats[h][1]).T * _silu(dg_ref[:, sl])).astype(o_ref.dtype)


def _attn_prompt(z_main, k_all, v_all, fcol, frow, *, layer_slot, n_batch, seq, tq, out_dtype):
    rows = z_main.shape[0]
    w = k_all.shape[2]
    n_q = seq // tq
    return pl.pallas_call(
        functools.partial(_attn_prompt_kernel, tq=tq),
        grid=(n_batch, n_q),
        in_specs=[
            pl.BlockSpec((tq, w), lambda b, i: (b * n_q + i, 2)),
            pl.BlockSpec((tq, w), lambda b, i: (b * n_q + i, 3)),
            pl.BlockSpec((1, seq, w), lambda b, i: (layer_slot, b, 0)),
            pl.BlockSpec((1, seq, w), lambda b, i: (layer_slot, b, 0)),
            pl.BlockSpec((seq, LANES), lambda b, i: (b, 0)),
            pl.BlockSpec((1, 1, D_HEADS, tq), lambda b, i: (b, i, 0, 0)),
        ],
        out_specs=pl.BlockSpec((tq, w), lambda b, i: (b * n_q + i, 0)),
        out_shape=jax.ShapeDtypeStruct((rows, w), out_dtype),
        scratch_shapes=[pltpu.VMEM((seq, w), BF16), pltpu.VMEM((n_q, w, tq), BF16),
                        pltpu.VMEM((D_HEADS, LANES, tq), BF16)]
        + [pltpu.VMEM((LANES, tq), F32)] * D_HEADS,
        compiler_params=_cparams("arbitrary", "arbitrary"),
        name="attn_prompt",
    )(z_main, z_main, k_all, v_all, fcol, frow)


PAGES_PER_STEP = 8
SAMPLE_SPLIT = 2


def _attn_sample_tables(n_new, g_pages):
    pg = np.arange(LANES)
    same_head = (pg[:, None] % D_HEADS) == (pg[None, :] % D_HEADS)
    later = (pg[:, None] // D_HEADS) > (pg[None, :] // D_HEADS)
    c_later = (same_head & later).astype(np.float32)
    c_same = same_head.astype(np.float32)
    rows = np.arange(g_pages * SUBLANES)
    r_later = (rows[None, :] > rows[:, None]).astype(np.float32)
    r = np.arange(D_HEADS * n_new)
    past_ok = (pg[None, :] % D_HEADS) == (r[:, None] // n_new)
    past_bias = np.where(past_ok, 0.0, NEG_BIG).astype(np.float32)
    cn = np.arange(n_new * D_HEADS)
    new_ok = ((cn[None, :] % D_HEADS) == (r[:, None] // n_new)) & ((cn[None, :] // D_HEADS) <= (r[:, None] % n_new))
    new_bias = np.where(new_ok, 0.0, NEG_BIG).astype(np.float32)
    m_cols = ((cn[:, None] % D_HEADS == cn[None, :] % D_HEADS)
              & (cn[:, None] // D_HEADS <= cn[None, :] // D_HEADS)).astype(np.float32)
    m_rows = ((cn[None, :] % D_HEADS == r[:, None] // n_new)
              & (cn[None, :] // D_HEADS <= r[:, None] % n_new)).astype(np.float32)
    return c_later, c_same, r_later, past_bias, new_bias, m_cols, m_rows


def _page_copies(pt_ref, ck_hbm, cv_hbm, clf_hbm, kbuf, vbuf, lfbuf, sems, step, slot, *,
                 layer_slot, n_groups, g_pages):
    b = step // n_groups
    first_page = (n_groups - 1 - step % n_groups) * g_pages
    copies = []
    for g in range(g_pages):
        page = pt_ref[b, first_page + g]
        copies.append(pltpu.make_async_copy(ck_hbm.at[layer_slot, page], kbuf.at[slot, g], sems.at[0, slot]))
        copies.append(pltpu.make_async_copy(cv_hbm.at[layer_slot, page], vbuf.at[slot, g], sems.at[1, slot]))
        copies.append(pltpu.make_async_copy(clf_hbm.at[layer_slot, page], lfbuf.at[slot, g], sems.at[2, slot]))
    return copies


def _attn_sample_kernel(pt_ref, q_ref, kn_ref, vn_ref, lfr_ref, lfc_ref, dg_ref,
                        cl_ref, cs_ref, rl_ref, pb_ref, nb_ref, mc_ref, mr_ref,
                        ck_hbm, cv_hbm, clf_hbm, o_ref,
                        kbuf, vbuf, lfbuf, sems, m_scr, l_scr, acc_scr, run_scr, base_scr, *,
                        layer_slot, n_groups, g_pages):
    step = pl.program_id(0)
    n_steps = pl.num_programs(0)
    slot = step % 2
    grp = step % n_groups
    scale = LANES ** -0.5
    hi = lax.Precision.HIGHEST
    copies = functools.partial(_page_copies, pt_ref, ck_hbm, cv_hbm, clf_hbm, kbuf, vbuf, lfbuf, sems,
                               layer_slot=layer_slot, n_groups=n_groups, g_pages=g_pages)

    @pl.when(step == 0)
    def _():
        for c in copies(0, 0):
            c.start()

    @pl.when(step + 1 < n_steps)
    def _():
        for c in copies(step + 1, 1 - slot):
            c.start()

    qb = q_ref[0].astype(BF16)
    rowc = jnp.sum(mr_ref[...] * lfr_ref[0], axis=-1, keepdims=True)

    @pl.when(grp == 0)
    def _():
        m_scr[...] = jnp.full(m_scr.shape, NEG_BIG, F32)
        l_scr[...] = jnp.zeros(l_scr.shape, F32)
        acc_scr[...] = jnp.zeros(acc_scr.shape, F32)
        run_scr[...] = jnp.zeros(run_scr.shape, F32)
        base_scr[...] = rowc + pb_ref[...]

    for c in copies(step, slot):
        c.wait()

    n_rows = g_pages * SUBLANES
    lf = lfbuf[slot].reshape(n_rows, LANES)
    within = jnp.dot(lf, cl_ref[...], precision=hi, preferred_element_type=F32)
    rowtot = jnp.dot(lf, cs_ref[...], precision=hi, preferred_element_type=F32)
    later_rows = jnp.dot(rl_ref[...], rowtot, precision=hi, preferred_element_type=F32)
    g_past = (within + later_rows) + run_scr[...]
    run_scr[...] = run_scr[...] + jnp.sum(rowtot, axis=0, keepdims=True)

    pg_rows = PAGE_SIZE * D_HEADS
    n_part = max(g_pages // SAMPLE_SPLIT, 1)
    parts = [(p0, min(p0 + n_part, g_pages)) for p0 in range(0, g_pages, n_part)]
    scores = [_dot_nt(qb, kbuf[slot, p0:p1].reshape((p1 - p0) * pg_rows, LANES).astype(BF16)) * scale
              for p0, p1 in parts]
    base = base_scr[...]
    m_run, l_run = m_scr[...], l_scr[...]
    for (p0, p1), s_part in zip(parts, scores):
        blocks = []
        s_max = None
        for j in range((p1 - p0) * SUBLANES):
            row = p0 * SUBLANES + j
            sj = (s_part[:, j * LANES:(j + 1) * LANES] + base) + g_past[row:row + 1, :]
            blocks.append(sj)
            s_max = sj if s_max is None else jnp.maximum(s_max, sj)
        m_new = jnp.maximum(m_run, jnp.max(s_max, axis=-1, keepdims=True))
        alpha = jnp.exp(m_run - m_new)
        p_sum = None
        p_blocks = []
        for sj in blocks:
            pj = jnp.exp(sj - m_new)
            p_sum = pj if p_sum is None else p_sum + pj
            p_blocks.append(pj.astype(BF16))
        l_run = alpha * l_run + jnp.sum(p_sum, axis=-1, keepdims=True)
        acc_scr[...] = alpha * acc_scr[...] + _dot(
            jnp.concatenate(p_blocks, axis=1),
            vbuf[slot, p0:p1].reshape((p1 - p0) * pg_rows, LANES).astype(BF16))
        m_run = m_new
    m_scr[...] = m_run
    l_scr[...] = l_run

    @pl.when(grp == n_groups - 1)
    def _():
        c_new = jnp.sum(mc_ref[...] * lfc_ref[0], axis=0, keepdims=True)
        s = (_dot_nt(qb, kn_ref[0].astype(BF16)) * scale + rowc) - c_new + nb_ref[...]
        m_old = m_scr[...]
        m_new = jnp.maximum(m_old, jnp.max(s, axis=-1, keepdims=True))
        alpha = jnp.exp(m_old - m_new)
        pr = jnp.exp(s - m_new)
        l_fin = alpha * l_scr[...] + jnp.sum(pr, axis=-1, keepdims=True)
        acc = alpha * acc_scr[...] + _dot(pr.astype(BF16), vn_ref[0].astype(BF16))
        o_ref[0] = (acc / l_fin) * _silu(dg_ref[0])


def _attn_sample(q_hq, kn, vn, lf_row, lf_col, dg_hq, cache_k, cache_v, cache_lf, page_table, *, layer_slot):
    bd, r, dh = q_hq.shape
    n_pages = page_table.shape[1]
    n_new = r // D_HEADS
    n_slots, n_phys = cache_k.shape[:2]
    pg_rows = PAGE_SIZE * D_HEADS
    ck = cache_k.reshape(n_slots, n_phys, pg_rows, dh)
    cv = cache_v.reshape(n_slots, n_phys, pg_rows, dh)
    clf = cache_lf.reshape(n_slots, n_phys, SUBLANES, LANES)
    g_pages = min(PAGES_PER_STEP, n_pages)
    assert n_pages % g_pages == 0
    n_groups = n_pages // g_pages
    tabs = [jnp.asarray(t) for t in _attn_sample_tables(n_new, g_pages)]

    def per_b(shape):
        return pl.BlockSpec((1,) + shape, lambda s, pt: (s // n_groups, 0, 0))

    def const(t):
        return pl.BlockSpec(t.shape, lambda s, pt: (0, 0))

    hbm = pl.BlockSpec(memory_space=pl.ANY)
    grid_spec = pltpu.PrefetchScalarGridSpec(
        num_scalar_prefetch=1,
        grid=(bd * n_groups,),
        in_specs=[per_b((r, dh)), per_b((r, dh)), per_b((r, dh)), per_b((1, r)), per_b((r, 1)), per_b((r, dh))]
        + [const(t) for t in tabs] + [hbm, hbm, hbm],
        out_specs=per_b((r, dh)),
        scratch_shapes=[
            pltpu.VMEM((2, g_pages, pg_rows, dh), F32), pltpu.VMEM((2, g_pages, pg_rows, dh), F32),
            pltpu.VMEM((2, g_pages, SUBLANES, LANES), F32), pltpu.SemaphoreType.DMA((3, 2)),
            pltpu.VMEM((r, 1), F32), pltpu.VMEM((r, 1), F32), pltpu.VMEM((r, dh), F32),
            pltpu.VMEM((1, LANES), F32), pltpu.VMEM((r, LANES), F32)],
    )
    return pl.pallas_call(
        functools.partial(_attn_sample_kernel, layer_slot=layer_slot, n_groups=n_groups, g_pages=g_pages),
        grid_spec=grid_spec,
        out_shape=jax.ShapeDtypeStruct((bd, r, dh), F32),
        compiler_params=_cparams("arbitrary"),
        name="attn_sample",
    )(page_table, q_hq, kn, vn, lf_row, lf_col, dg_hq, *tabs, ck, cv, clf)


TM_NORM = 1024
TM_IN = 2048
TN_IN_EVEN = 512
TN_IN_ODD = 256
TN_IN_SAMPLE = 1024
TM_OUT = 512
TQ_POOL = 256
TQ_ATTN = 256


def kernel(x_prompt, x_sample, c_prompt, c_sample, state_ret, state_pool, cache_k, cache_v, cache_logf,
           page_table, g_pre, g_post, w_ada, b_ada, w_in_even, w_out_even, ln_a_g, ln_a_b, w_s, b_s,
           ret_g, w_in_odd, b_f, w_out_odd, w_pool, pool_scale):
    bp, seq, d = x_prompt.shape
    bd, n_new, _ = x_sample.shape
    depth = g_pre.shape[0]
    n_odd = w_in_odd.shape[0]
    w = d // 2
    n_past = page_table.shape[1] * PAGE_SIZE
    rows_s = bd * n_new

    c_all = jnp.concatenate([c_prompt, c_sample], axis=0)
    c_all = jnp.pad(c_all, ((0, -c_all.shape[0] % (2 * SUBLANES)), (0, 0)))
    mod = _ada_mod(c_all, w_ada, b_ada)

    def mods(l):
        mp = mod[l, :bp].reshape(bp, 1, 3 * d)
        ms = jnp.repeat(mod[l, bp:bp + bd], n_new, axis=0).reshape(1, rows_s, 3 * d)
        return [(m[..., :d], m[..., d:2 * d], m[..., 2 * d:]) for m in (mp, ms)]

    xp = x_prompt.reshape(bp * seq, d)
    xs = x_sample.reshape(rows_s, d)
    ret_p, ret_s, gv_s, pool_p, pool_s, lfp_l, lfs_l = [], [], [], [], [], [], []
    kv_p = kv_s = None
    zeros_state = jnp.zeros((1, bp) + state_ret.shape[2:], F32)
    w_out_even_bf, w_out_odd_bf, w_pool_bf = w_out_even.astype(BF16), w_out_odd.astype(BF16), w_pool.astype(BF16)
    w_in_odd_t = jnp.swapaxes(w_in_odd, 1, 2)
    all_mods = [mods(l) for l in range(depth)]
    (shift_p, scale_p, _), (shift_s, scale_s, _) = all_mods[0]
    g0 = g_pre[0].reshape(1, d)
    hp = _modnorm_call(xp, scale_p, shift_p, g0, per_row=False, rows_per_mod=seq, tm=TM_NORM)
    hs = _modnorm_call(xs, scale_s, shift_s, g0, per_row=True, rows_per_mod=1, tm=rows_s)
    for l in range(depth):
        j = l // 2
        (_, _, gate_p), (_, _, gate_s) = all_mods[l]
        gpost = g_post[l].reshape(1, d)
        if l % 2 == 0:
            w_out = w_out_even_bf
            zp = _inproj_even(hp, w_in_even, layer_slot=j, tm=TM_IN, tn=TN_IN_EVEN)
            zs = _inproj_even(hs, w_in_even, layer_slot=j, tm=rows_s, tn=TN_IN_SAMPLE)
            mix = functools.partial(_even_mixer, ln_g=ln_a_g[j], ln_b=ln_a_b[j], w_s=w_s[j], b_s=b_s[j],
                                    ret_g=ret_g[j])
            ap, rp, sp = mix(zp, zeros_state, s0_slot=0, n_batch=bp, seq=seq, pos0=0, emit_vn=False,
                             out_dtype=BF16)
            as_, rs, ss, vn_s = mix(zs, state_ret, s0_slot=j, n_batch=bd, seq=n_new, pos0=n_past, emit_vn=True,
                                    out_dtype=F32)
            ret_p.append(sp)
            ret_s.append(ss)
            gv_s.append(vn_s.reshape(bd, n_new, w))
        else:
            wf = jnp.pad(w_in_odd_t[j, 6 * w:, :], ((0, LANES - D_HEADS), (0, 0)))
            bf = jnp.pad(b_f[j], (0, LANES - D_HEADS)).reshape(1, LANES)
            w_out = w_out_odd_bf
            wp = w_pool_bf
            zp, kp_all, vp_all, lfp = _inproj_odd(hp, w_in_odd_t, wf, bf, kv_p, layer_slot=j, n_slots=n_odd,
                                                  tm=TM_IN, tn=TN_IN_ODD)
            kv_p = (kp_all, vp_all)
            zs, ks_all, vs_all, lfs = _inproj_odd(hs, w_in_odd_t, wf, bf, kv_s, layer_slot=j, n_slots=n_odd,
                                                  tm=rows_s, tn=TN_IN_SAMPLE)
            kv_s = (ks_all, vs_all)
            ap = _pool_branch(zp, jnp.zeros((bp, POOL_PREV, w), F32), _pool_counts(0, seq), wp, pool_scale[j],
                              layer_slot=j, n_batch=bp, seq=seq, tq=TQ_POOL, out_dtype=BF16)
            fcol, frow = _fcum(lfp, n_batch=bp, seq=seq, blk=TQ_ATTN)
            rp = _attn_prompt(zp, kp_all, vp_all, fcol, frow, layer_slot=j, n_batch=bp, seq=seq, tq=TQ_ATTN,
                              out_dtype=BF16)
            pool_p.append(zp.reshape(bp, seq, 4 * w)[:, seq - POOL_BUF:, :w])
            lfp_l.append(lfp[:, :D_HEADS].reshape(bp, seq, D_HEADS))
            prev = jnp.pad(state_pool[j], ((0, 0), (POOL_PREV - POOL_BUF, 0), (0, 0)))
            as_ = _pool_branch(zs, prev, _pool_counts(n_past, POOL_PREV), wp, pool_scale[j],
                               layer_slot=j, n_batch=bd, seq=n_new, tq=n_new, out_dtype=F32)
            zs4 = zs.reshape(bd, n_new, 4, D_HEADS, LANES)
            to_hq = lambda a: a.transpose(0, 2, 1, 3).reshape(bd, D_HEADS * n_new, LANES)
            lf_new = lfs[:, :D_HEADS].reshape(bd, n_new * D_HEADS)
            o_hq = _attn_sample(to_hq(zs4[:, :, 2]), ks_all[j].reshape(bd, n_new * D_HEADS, LANES),
                                vs_all[j].reshape(bd, n_new * D_HEADS, LANES),
                                lf_new.reshape(bd, 1, -1), lf_new.reshape(bd, -1, 1), to_hq(zs4[:, :, 3]),
                                cache_k, cache_v, cache_logf, page_table, layer_slot=j)
            rs = o_hq.reshape(bd, D_HEADS, n_new, LANES).transpose(0, 2, 1, 3).reshape(rows_s, w)
            c_in_s = zs[:, :w].reshape(bd, n_new, w)
            pool_s.append(jnp.concatenate([state_pool[j], c_in_s], axis=1)[:, -POOL_BUF:])
            lfs_l.append(lfs[:, :D_HEADS].reshape(bd, n_new, D_HEADS))
        next_p = next_s = None
        if l + 1 < depth:
            (shift_p, scale_p, _), (shift_s, scale_s, _) = all_mods[l + 1]
            g_next = g_pre[l + 1].reshape(1, d)
            next_p, next_s = (scale_p, shift_p, g_next), (scale_s, shift_s, g_next)
        xp, hp = _outproj(ap, rp, xp, gate_p, gpost, w_out, next_p, layer_slot=j, per_row=False,
                          rows_per_mod=seq, tm=TM_OUT)
        xs, hs = _outproj(as_, rs, xs, gate_s, gpost, w_out, next_s, layer_slot=j, per_row=True,
                          rows_per_mod=1, tm=rows_s)

    dh = LANES
    return (xp.reshape(bp, seq, d), xs.reshape(bd, n_new, d),
            jnp.stack(ret_p), jnp.stack(ret_s), jnp.stack(gv_s), jnp.stack(pool_p), jnp.stack(pool_s),
            kv_p[0].reshape(n_odd, bp, seq, D_HEADS, dh), kv_p[1].reshape(n_odd, bp, seq, D_HEADS, dh),
            jnp.stack(lfp_l),
            kv_s[0].reshape(n_odd, bd, n_new, D_HEADS, dh), kv_s[1].reshape(n_odd, bd, n_new, D_HEADS, dh),
            jnp.stack(lfs_l))
```

```python
import functools

import numpy as np
import jax
import jax.numpy as jnp
from jax import lax
from jax.experimental import pallas as pl
from jax.experimental.pallas import tpu as pltpu

F32 = jnp.float32
BF16 = jnp.bfloat16

EPS = 1e-6
ROPE_BASE = 10000.0
CHUNK = 128
A_GROUPS = 8
B_HEADS = 4
POOL_WINDOWS = (2, 4, 8, 16)
POOL_BUF = 15
D_HEADS = 8
PAGE_SIZE = 128
LANES = 128
SUBLANES = 8
NEG_BIG = -1e30
VMEM_LIMIT = 56 * 1024 * 1024


def _cparams(*sem):
    return pltpu.CompilerParams(dimension_semantics=sem, vmem_limit_bytes=VMEM_LIMIT)


def _silu(x):
    return x * jax.nn.sigmoid(x)


def _log_sigmoid(x):
    return jnp.minimum(x, 0.0) - jnp.log1p(jnp.exp(-jnp.abs(x)))


def _dot(a, b):
    return jnp.dot(a, b, preferred_element_type=F32)


def _dot_nt(a, b):
    return lax.dot_general(a, b, (((1,), (1,)), ((), ())), preferred_element_type=F32)


def _dot_tn(a, b):
    return lax.dot_general(a, b, (((0,), (0,)), ((), ())), preferred_element_type=F32)


def _ada_kernel(c_ref, w_ref, b_ref, o_ref):
    a = _silu(c_ref[...]).astype(BF16)
    o_ref[0] = _dot(a, w_ref[0].astype(BF16)) + b_ref[0]


def _ada_mod(c_all, w_ada, b_ada, tn=1024):
    depth, d, n = w_ada.shape
    r = c_all.shape[0]
    return pl.pallas_call(
        _ada_kernel,
        grid=(depth, n // tn),
        in_specs=[
            pl.BlockSpec((r, d), lambda l, j: (0, 0)),
            pl.BlockSpec((1, d, tn), lambda l, j: (l, 0, j)),
            pl.BlockSpec((1, 1, tn), lambda l, j: (l, 0, j)),
        ],
        out_specs=pl.BlockSpec((1, r, tn), lambda l, j: (l, 0, j)),
        out_shape=jax.ShapeDtypeStruct((depth, r, n), F32),
        compiler_params=_cparams("arbitrary", "arbitrary"),
        name="ada_mod",
    )(c_all, w_ada, b_ada.reshape(depth, 1, n))


def _modnorm(x, scale, shift, g):
    y = x * lax.rsqrt(jnp.mean(x * x, axis=-1, keepdims=True) + EPS) * g
    return (y * (1.0 + scale) + shift).astype(BF16)


def _modnorm_kernel(x_ref, sc_ref, sh_ref, g_ref, h_ref):
    h_ref[...] = _modnorm(x_ref[...], sc_ref[0], sh_ref[0], g_ref[...])


def _mod_spec(per_row, rows_per_mod, tm, d):
    if per_row:
        return pl.BlockSpec((1, tm, d), lambda i: (0, i, 0))
    return pl.BlockSpec((1, 1, d), lambda i: ((i * tm) // rows_per_mod, 0, 0))


def _modnorm_call(x2d, scale, shift, g, *, per_row, rows_per_mod, tm):
    rows, d = x2d.shape
    return pl.pallas_call(
        _modnorm_kernel,
        grid=(rows // tm,),
        in_specs=[pl.BlockSpec((tm, d), lambda i: (i, 0)),
                  _mod_spec(per_row, rows_per_mod, tm, d), _mod_spec(per_row, rows_per_mod, tm, d),
                  pl.BlockSpec((1, d), lambda i: (0, 0))],
        out_specs=pl.BlockSpec((tm, d), lambda i: (i, 0)),
        out_shape=jax.ShapeDtypeStruct((rows, d), BF16),
        compiler_params=_cparams("arbitrary"),
        name="modnorm",
    )(x2d, scale, shift, g)


def _inproj_even_kernel(h_ref, w_ref, z_ref):
    z_ref[...] = _dot(h_ref[...], w_ref[0].astype(BF16))


def _inproj_odd_kernel(h_ref, w_ref, wf_ref, bf_ref, *rest, tiles_per_group, kv_slot):
    z_ref, k_ref, v_ref, lf_ref = rest[-4:]
    j = pl.program_id(1)

    def project():
        return _dot_nt(h_ref[...], w_ref[0].astype(BF16))

    def put(kv_ref):
        for s in range(kv_ref.shape[0]):
            kv_ref[s] = project() if s == kv_slot else jnp.zeros(kv_ref.shape[1:], F32)

    @pl.when(j == 0)
    def _():
        lf_ref[...] = _log_sigmoid(_dot_nt(h_ref[...], wf_ref[...].astype(BF16)) + bf_ref[...])

    group = j // tiles_per_group

    @pl.when(jnp.logical_or(group < 3, group == 5))
    def _():
        z_ref[...] = project()

    @pl.when(group == 3)
    def _():
        put(k_ref)

    @pl.when(group == 4)
    def _():
        put(v_ref)


def _inproj_even(h, w_all, *, layer_slot, tm, tn):
    rows, d = h.shape
    n = w_all.shape[2]
    return pl.pallas_call(
        _inproj_even_kernel,
        grid=(rows // tm, n // tn),
        in_specs=[
            pl.BlockSpec((tm, d), lambda i, j: (i, 0)),
            pl.BlockSpec((1, d, tn), lambda i, j: (layer_slot, 0, j)),
        ],
        out_specs=pl.BlockSpec((tm, tn), lambda i, j: (i, j)),
        out_shape=jax.ShapeDtypeStruct((rows, n), F32),
        compiler_params=_cparams("arbitrary", "arbitrary"),
        name="inproj_even",
    )(h, w_all)


def _inproj_odd(h, w_all_t, wf, bf, kv_prev, *, layer_slot, n_slots, tm, tn):
    rows, d = h.shape
    w = (w_all_t.shape[1] - D_HEADS) // 6
    tpg = w // tn
    n_tiles = 6 * tpg

    creates_kv = kv_prev is None
    kv_block = (n_slots if creates_kv else 1, tm, tn)
    kv_first = 0 if creates_kv else layer_slot

    def zmap(i, j):
        return (i, j - jnp.clip(j - (3 * tpg - 1), 0, 2 * tpg))

    def kmap(i, j):
        return (kv_first, i, jnp.clip(j - 3 * tpg, 0, tpg - 1))

    def vmap(i, j):
        return (kv_first, i, jnp.clip(j - 4 * tpg, 0, tpg - 1))

    in_specs = [
        pl.BlockSpec((tm, d), lambda i, j: (i, 0)),
        pl.BlockSpec((1, tn, d), lambda i, j: (layer_slot, j, 0)),
        pl.BlockSpec((LANES, d), lambda i, j: (0, 0)),
        pl.BlockSpec((1, LANES), lambda i, j: (0, 0)),
    ]
    args = [h, w_all_t, wf, bf]
    aliases = {}
    if kv_prev is not None:
        in_specs += [pl.BlockSpec(memory_space=pl.ANY)] * 2
        aliases = {len(args): 1, len(args) + 1: 2}
        args += list(kv_prev)
    kv_shape = jax.ShapeDtypeStruct((n_slots, rows, w), F32)
    return pl.pallas_call(
        functools.partial(_inproj_odd_kernel, tiles_per_group=tpg, kv_slot=layer_slot if creates_kv else 0),
        grid=(rows // tm, n_tiles),
        in_specs=in_specs,
        out_specs=[
            pl.BlockSpec((tm, tn), zmap),
            pl.BlockSpec(kv_block, kmap),
            pl.BlockSpec(kv_block, vmap),
            pl.BlockSpec((tm, LANES), lambda i, j: (i, 0)),
        ],
        out_shape=[
            jax.ShapeDtypeStruct((rows, 4 * w), F32),
            kv_shape,
            kv_shape,
            jax.ShapeDtypeStruct((rows, LANES), F32),
        ],
        input_output_aliases=aliases,
        compiler_params=_cparams("arbitrary", "arbitrary"),
        name="inproj_odd",
    )(*args)


def _outproj_kernel(a_ref, b_ref, x_ref, gate_ref, g_ref, w1_ref, w2_ref, *rest, with_next):
    y = _dot(a_ref[...].astype(BF16), w1_ref[0]) + _dot(b_ref[...].astype(BF16), w2_ref[0])
    yn = y * lax.rsqrt(jnp.mean(y * y, axis=-1, keepdims=True) + EPS) * g_ref[...]
    x_new = x_ref[...] + gate_ref[0] * yn
    if with_next:
        sc_ref, sh_ref, gn_ref, o_ref, h_ref = rest
        h_ref[...] = _modnorm(x_new, sc_ref[0], sh_ref[0], gn_ref[...])
    else:
        o_ref, = rest
    o_ref[...] = x_new


def _outproj(a, b, x2d, gate, g, w_bf, next_mod, *, layer_slot, per_row, rows_per_mod, tm):
    rows, d = x2d.shape
    w = a.shape[1]
    mod_spec = _mod_spec(per_row, rows_per_mod, tm, d)
    row_spec = pl.BlockSpec((tm, d), lambda i: (i, 0))
    vec_spec = pl.BlockSpec((1, d), lambda i: (0, 0))
    in_specs = [
        pl.BlockSpec((tm, w), lambda i: (i, 0)),
        pl.BlockSpec((tm, w), lambda i: (i, 0)),
        row_spec, mod_spec, vec_spec,
        pl.BlockSpec((1, w, d), lambda i: (layer_slot, 0, 0)),
        pl.BlockSpec((1, w, d), lambda i: (layer_slot, 1, 0)),
    ]
    args = [a, b, x2d, gate, g, w_bf, w_bf]
    out_specs = [row_spec]
    out_shape = [jax.ShapeDtypeStruct((rows, d), F32)]
    if next_mod is not None:
        in_specs += [mod_spec, mod_spec, vec_spec]
        args += list(next_mod)
        out_specs.append(row_spec)
        out_shape.append(jax.ShapeDtypeStruct((rows, d), BF16))
    res = pl.pallas_call(
        functools.partial(_outproj_kernel, with_next=next_mod is not None),
        grid=(rows // tm,),
        in_specs=in_specs,
        out_specs=out_specs,
        out_shape=out_shape,
        compiler_params=_cparams("arbitrary"),
        name="outproj",
    )(*args)
    return res if next_mod is not None else (res[0], None)


def _retention_tables(c_len, pos0, n_rows):
    lg = np.log(1.0 - 2.0 ** (-5.0 - np.arange(B_HEADS, dtype=np.float64)))
    t = np.arange(CHUNK, dtype=np.float64)
    diff = t[:, None] - t[None, :]
    dmask = np.where(diff >= 0, np.exp(lg[:, None, None] * np.maximum(diff, 0.0)), 0.0).astype(np.float32)
    qdec = np.exp(lg[None, :] * (t + 1.0)[:, None]).astype(np.float32)
    kdec = np.exp(lg[None, :] * np.maximum(c_len - 1.0 - t, 0.0)[:, None]).astype(np.float32)
    cdec = tuple(float(v) for v in np.exp(lg * c_len).astype(np.float32))
    half = LANES
    inv = ROPE_BASE ** (-np.arange(half, dtype=np.float64) / half)
    pos = (pos0 + np.arange(n_rows)).astype(np.float64)
    ang = pos[:, None] * inv[None, :]
    return dmask, qdec, kdec, cdec, ang


MIXER_CHUNKS_PER_STEP = 4


def _even_mixer_kernel(u_ref, v_ref, ga_ref, q_ref, k_ref, vv_ref, gr_ref,
                       lng_ref, lnb_ref, ws_ref, bst_ref, retg_ref, cos_ref, sin_ref,
                       dmask_ref, qdec_ref, kdec_ref, s0_ref, *rest, c_in, cdec, emit_vn):
    if emit_vn:
        ao_ref, ro_ref, so_ref, vn_ref, s_scr = rest
    else:
        ao_ref, ro_ref, so_ref, s_scr = rest
    c = pl.program_id(1)

    @pl.when(c == 0)
    def _():
        s_scr[...] = s0_ref[0, 0]

    n_sub = max(u_ref.shape[0] // CHUNK, 1)
    w = u_ref.shape[1]
    gw = w // A_GROUPS
    dk = w // B_HEADS
    half = dk // 2
    causal = (lax.broadcasted_iota(jnp.int32, (CHUNK, CHUNK), 0)
              >= lax.broadcasted_iota(jnp.int32, (CHUNK, CHUNK), 1))
    wms = [jnp.where(causal, ws_ref[g], 0.0).astype(BF16) for g in range(A_GROUPS)]
    for cc in range(n_sub):
        r0 = cc * CHUNK
        n_out = min(c_in, CHUNK)
        rows = slice(r0, r0 + n_out)

        def ld(ref):
            if c_in >= CHUNK:
                return ref[r0:r0 + CHUNK, :]
            return jnp.concatenate([ref[...], jnp.zeros((CHUNK - c_in, w), F32)], axis=0)

        v = ld(v_ref)
        xc = v - jnp.mean(v, axis=-1, keepdims=True)
        vn = xc * lax.rsqrt(jnp.mean(xc * xc, axis=-1, keepdims=True) + EPS) * lng_ref[...] + lnb_ref[...]
        if emit_vn:
            vn_ref[rows, :] = vn[:n_out]
        vnb = vn.astype(BF16)
        u = ld(u_ref)
        ga = ld(ga_ref)
        for g in range(A_GROUPS):
            sl = slice(g * gw, (g + 1) * gw)
            mixed = _dot(wms[g], vnb[:, sl]) + bst_ref[:, g:g + 1]
            ao_ref[rows, sl] = (u[:, sl] * mixed * _silu(ga[:, sl]))[:n_out].astype(ao_ref.dtype)

        cos = cos_ref[r0:r0 + CHUNK, :]
        sin = sin_ref[r0:r0 + CHUNK, :]
        q = ld(q_ref)
        k = ld(k_ref)
        vv = ld(vv_ref)
        gr = ld(gr_ref)
        for h in range(B_HEADS):
            sl = slice(h * dk, (h + 1) * dk)
            q1, q2 = q[:, h * dk:h * dk + half], q[:, h * dk + half:(h + 1) * dk]
            k1, k2 = k[:, h * dk:h * dk + half], k[:, h * dk + half:(h + 1) * dk]
            qr = jnp.concatenate([q1 * cos - q2 * sin, q1 * sin + q2 * cos], axis=1)
            kr = jnp.concatenate([k1 * cos - k2 * sin, k1 * sin + k2 * cos], axis=1) * (dk ** -0.5)
            qrb = qr.astype(BF16)
            vb = vv[:, sl].astype(BF16)
            inner = _dot_nt(qrb, kr.astype(BF16)) * dmask_ref[h]
            s = s_scr[h]
            o = _dot(inner.astype(BF16), vb) + _dot(qrb, s.astype(BF16)) * qdec_ref[:, h:h + 1]
            kd = (kr * kdec_ref[:, h:h + 1]).astype(BF16)
            s_scr[h] = s * cdec[h] + _dot_tn(kd, vb)
            on = o * lax.rsqrt(jnp.mean(o * o, axis=-1, keepdims=True) + EPS) * retg_ref[:, sl]
            ro_ref[rows, sl] = (on * _silu(gr[:, sl]))[:n_out].astype(ro_ref.dtype)

    @pl.when(c == pl.num_programs(1) - 1)
    def _():
        so_ref[0] = s_scr[...]


def _even_mixer(z, s0_all, ln_g, ln_b, w_s, b_s, ret_g, *, s0_slot, n_batch, seq, pos0, emit_vn, out_dtype):
    rows, n = z.shape
    s_shape = s0_all.shape[2:]
    w = n // 7
    c_len = CHUNK if seq % CHUNK == 0 else seq
    n_chunks_total = seq // min(seq, CHUNK)
    per_step = min(MIXER_CHUNKS_PER_STEP, n_chunks_total)
    assert n_chunks_total % per_step == 0
    c_in = min(seq, CHUNK) * per_step
    n_chunks = n_chunks_total // per_step
    dmask, qdec, kdec, cdec, ang = _retention_tables(c_len, pos0, n_chunks_total * CHUNK)
    cos = jnp.asarray(np.cos(ang).astype(np.float32))
    sin = jnp.asarray(np.sin(ang).astype(np.float32))

    def zspec(col):
        return pl.BlockSpec((c_in, w), lambda b, c, col=col: (b * n_chunks + c, col))

    def const(shape):
        return pl.BlockSpec(shape, lambda b, c: (0,) * len(shape))

    row_spec = pl.BlockSpec((c_in, w), lambda b, c: (b * n_chunks + c, 0))
    state_in_spec = pl.BlockSpec((1, 1) + s_shape, lambda b, c: (s0_slot, b, 0, 0, 0))
    state_spec = pl.BlockSpec((1,) + s_shape, lambda b, c: (b, 0, 0, 0))
    out_specs = [row_spec, row_spec, state_spec]
    out_shape = [jax.ShapeDtypeStruct((rows, w), out_dtype),
                 jax.ShapeDtypeStruct((rows, w), out_dtype),
                 jax.ShapeDtypeStruct((n_batch,) + s_shape, F32)]
    if emit_vn:
        out_specs.append(row_spec)
        out_shape.append(jax.ShapeDtypeStruct((rows, w), F32))
    return pl.pallas_call(
        functools.partial(_even_mixer_kernel, c_in=c_in, cdec=cdec, emit_vn=emit_vn),
        grid=(n_batch, n_chunks),
        in_specs=[zspec(i) for i in range(7)] + [
            const((1, w)), const((1, w)), const(w_s.shape), const((CHUNK, A_GROUPS)), const((1, w)),
            pl.BlockSpec((per_step * CHUNK, LANES), lambda b, c: (c, 0)),
            pl.BlockSpec((per_step * CHUNK, LANES), lambda b, c: (c, 0)),
            const(dmask.shape), const(qdec.shape), const(kdec.shape),
            state_in_spec,
        ],
        out_specs=out_specs,
        out_shape=out_shape,
        scratch_shapes=[pltpu.VMEM(s_shape, F32)],
        compiler_params=_cparams("arbitrary", "arbitrary"),
        name="even_mixer",
    )(z, z, z, z, z, z, z, ln_g.reshape(1, w), ln_b.reshape(1, w), w_s, b_s.T, ret_g.reshape(1, w),
      cos, sin, jnp.asarray(dmask), jnp.asarray(qdec), jnp.asarray(kdec), s0_all)


POOL_PREV = 16
POOL_LEAD = SUBLANES


def _pool_kernel(cin_ref, cg_ref, prev_ref, cnt_ref, wp_ref, ps_ref, o_ref, e0, e1, e2, e3, e4, *, tq, t_pad):
    t = pl.program_id(1)
    base = POOL_LEAD + POOL_PREV
    total = base + t_pad
    w = cin_ref.shape[1]
    gw = w // len(POOL_WINDOWS)

    @pl.when(t == 0)
    def _():
        for e in (e0, e1, e2, e3, e4):
            e[0:POOL_LEAD, :] = jnp.zeros((POOL_LEAD, w), F32)
        e0[POOL_LEAD:base, :] = prev_ref[0]
        if t_pad > tq:
            e0[base + tq:total, :] = jnp.zeros((t_pad - tq, w), F32)

    e0[base:base + tq, :] = cin_ref[...]
    levels = (e0, e1, e2, e3, e4)
    for lvl in range(1, 5):
        shift = 1 << (lvl - 1)
        c0 = (lvl - 1) * gw
        src, dst = levels[lvl - 1], levels[lvl]
        dst[POOL_LEAD:total, c0:] = (src[POOL_LEAD:total, c0:]
                                     + src[POOL_LEAD - shift:total - shift, c0:])
    cur = e0[base:total, :]
    cg = cg_ref[...]
    for gi in range(len(POOL_WINDOWS)):
        sl = slice(gi * gw, (gi + 1) * gw)
        win = levels[gi + 1][base:total, sl]
        pooled = win / cnt_ref[:, gi:gi + 1] - cur[:, sl]
        mixed = _dot(pooled.astype(BF16), wp_ref[0, gi]) * ps_ref[:, sl]
        o_ref[:, sl] = (mixed[:tq] * _silu(cg[:, sl])).astype(o_ref.dtype)
    e0[POOL_LEAD:base, :] = e0[POOL_LEAD + t_pad:base + t_pad, :]


def _pool_branch(z_main, prev, cnt, wp_bf, pool_scale, *, layer_slot, n_batch, seq, tq, out_dtype):
    rows = z_main.shape[0]
    w = wp_bf.shape[1] * wp_bf.shape[2]
    t_pad = max(tq, POOL_PREV)
    n_t = seq // tq
    ext = pltpu.VMEM((POOL_LEAD + POOL_PREV + t_pad, w), F32)
    return pl.pallas_call(
        functools.partial(_pool_kernel, tq=tq, t_pad=t_pad),
        grid=(n_batch, n_t),
        in_specs=[
            pl.BlockSpec((tq, w), lambda b, t: (b * n_t + t, 0)),
            pl.BlockSpec((tq, w), lambda b, t: (b * n_t + t, 1)),
            pl.BlockSpec((1, POOL_PREV, w), lambda b, t: (b, 0, 0)),
            pl.BlockSpec((t_pad, len(POOL_WINDOWS)), lambda b, t: (t, 0)),
            pl.BlockSpec((1,) + wp_bf.shape[1:], lambda b, t: (layer_slot, 0, 0, 0)),
            pl.BlockSpec((1, w), lambda b, t: (0, 0)),
        ],
        out_specs=pl.BlockSpec((tq, w), lambda b, t: (b * n_t + t, 0)),
        out_shape=jax.ShapeDtypeStruct((rows, w), out_dtype),
        scratch_shapes=[ext] * 5,
        compiler_params=_cparams("arbitrary", "arbitrary"),
        name="pool_branch",
    )(z_main, z_main, prev, cnt, wp_bf, pool_scale.reshape(1, w))


def _pool_counts(pos0, n_rows):
    pos = pos0 + np.arange(n_rows)
    return jnp.asarray(np.stack([np.minimum(pos + 1, wd) for wd in POOL_WINDOWS], axis=1).astype(np.float32))


def _split3(x):
    a = x.astype(BF16)
    r1 = x - a.astype(F32)
    b = r1.astype(BF16)
    c = (r1 - b.astype(F32)).astype(BF16)
    return a, b, c


def _fcum_kernel(lf_ref, fc_ref, fr_ref, *, blk):
    n_blk = lf_ref.shape[0] // blk
    tri = (lax.broadcasted_iota(jnp.int32, (blk, blk), 0)
           >= lax.broadcasted_iota(jnp.int32, (blk, blk), 1)).astype(BF16)
    carry = jnp.zeros((1, LANES), F32)
    for i in range(n_blk):
        a, b, c = _split3(lf_ref[i * blk:(i + 1) * blk, :])
        cs = (_dot(tri, a) + _dot(tri, b)) + _dot(tri, c) + carry
        carry = cs[blk - 1:blk, :]
        fc_ref[i * blk:(i + 1) * blk, :] = cs
        fr_ref[0, i] = cs.T[:D_HEADS, :]


def _fcum(lf, *, n_batch, seq, blk):
    return pl.pallas_call(
        functools.partial(_fcum_kernel, blk=blk),
        grid=(n_batch,),
        in_specs=[pl.BlockSpec((seq, LANES), lambda b: (b, 0))],
        out_specs=[pl.BlockSpec((seq, LANES), lambda b: (b, 0)),
                   pl.BlockSpec((1, seq // blk, D_HEADS, blk), lambda b: (b, 0, 0, 0))],
        out_shape=[jax.ShapeDtypeStruct((n_batch * seq, LANES), F32),
                   jax.ShapeDtypeStruct((n_batch, seq // blk, D_HEADS, blk), F32)],
        compiler_params=_cparams("arbitrary"),
        name="forget_cumsum",
    )(lf)


SCORE_LEAD = 3


def _attn_prompt_kernel(q_ref, dg_ref, k_ref, v_ref, fc_ref, fr_ref, o_ref,
                        kb_scr, vt_scr, qt_scr, *acc_scrs, tq):
    i = pl.program_id(1)
    dh = LANES
    scale = dh ** -0.5
    n_blk = k_ref.shape[1] // tq

    @pl.when(i == 0)
    def _():
        kb_scr[...] = k_ref[0].astype(BF16)
        for h in range(D_HEADS):
            for kb in range(n_blk):
                vt_scr[kb, h * dh:(h + 1) * dh, :] = (
                    v_ref[0, kb * tq:(kb + 1) * tq, h * dh:(h + 1) * dh].T.astype(BF16))

    for h in range(D_HEADS):
        qt_scr[h] = q_ref[:, h * dh:(h + 1) * dh].T.astype(BF16)
        acc_scrs[h][...] = jnp.zeros((dh, tq), F32)
    key_le_query = (lax.broadcasted_iota(jnp.int32, (tq, tq), 0)
                    <= lax.broadcasted_iota(jnp.int32, (tq, tq), 1))

    def block(kb, stats, masked):
        r0 = pl.multiple_of(kb * tq, tq)
        new_stats = []

        def scores(h):
            sl = slice(h * dh, (h + 1) * dh)
            return _dot(kb_scr[pl.ds(r0, tq), sl], qt_scr[h]) * scale - fc_ref[pl.ds(r0, tq), h:h + 1]

        pending = [scores(h) for h in range(SCORE_LEAD)]
        for h in range(D_HEADS):
            sl = slice(h * dh, (h + 1) * dh)
            m_old, l_old = stats[h]
            fq = fr_ref[0, 0, h:h + 1, :]
            u = pending.pop(0)
            if h + SCORE_LEAD < D_HEADS:
                pending.append(scores(h + SCORE_LEAD))
            if masked:
                u = jnp.where(key_le_query, u, NEG_BIG)
            m_new = jnp.maximum(m_old, jnp.max(u, axis=0, keepdims=True) + fq)
            alpha = jnp.exp(m_old - m_new)
            p = jnp.exp(u - (m_new - fq))
            new_stats.append((m_new, alpha * l_old + jnp.sum(p, axis=0, keepdims=True)))
            acc_scrs[h][...] = alpha * acc_scrs[h][...] + _dot(vt_scr[kb, sl, :], p.astype(BF16))
        return tuple(new_stats)

    stats0 = tuple((jnp.full((1, tq), NEG_BIG, F32), jnp.zeros((1, tq), F32)) for _ in range(D_HEADS))
    stats = lax.fori_loop(0, i, lambda kb, st: block(kb, st, False), stats0)
    stats = block(i, stats, True)
    for h in range(D_HEADS):
        sl = slice(h * dh, (h + 1) * dh)
        o_ref[:, sl] = ((acc_scrs[h][...] / stats[h][1]).T * _silu(dg_ref[:, sl])).astype(o_ref.dtype)


def _attn_prompt(z_main, k_all, v_all, fcol, frow, *, layer_slot, n_batch, seq, tq, out_dtype):
    rows = z_main.shape[0]
    w = k_all.shape[2]
    n_q = seq // tq
    return pl.pallas_call(
        functools.partial(_attn_prompt_kernel, tq=tq),
        grid=(n_batch, n_q),
        in_specs=[
            pl.BlockSpec((tq, w), lambda b, i: (b * n_q + i, 2)),
            pl.BlockSpec((tq, w), lambda b, i: (b * n_q + i, 3)),
            pl.BlockSpec((1, seq, w), lambda b, i: (layer_slot, b, 0)),
            pl.BlockSpec((1, seq, w), lambda b, i: (layer_slot, b, 0)),
            pl.BlockSpec((seq, LANES), lambda b, i: (b, 0)),
            pl.BlockSpec((1, 1, D_HEADS, tq), lambda b, i: (b, i, 0, 0)),
        ],
        out_specs=pl.BlockSpec((tq, w), lambda b, i: (b * n_q + i, 0)),
        out_shape=jax.ShapeDtypeStruct((rows, w), out_dtype),
        scratch_shapes=[pltpu.VMEM((seq, w), BF16), pltpu.VMEM((n_q, w, tq), BF16),
                        pltpu.VMEM((D_HEADS, LANES, tq), BF16)]
        + [pltpu.VMEM((LANES, tq), F32)] * D_HEADS,
        compiler_params=_cparams("arbitrary", "arbitrary"),
        name="attn_prompt",
    )(z_main, z_main, k_all, v_all, fcol, frow)


PAGES_PER_STEP = 16
SAMPLE_SPLIT = 4


def _attn_sample_tables(n_new, g_pages):
    pg = np.arange(LANES)
    same_head = (pg[:, None] % D_HEADS) == (pg[None, :] % D_HEADS)
    later = (pg[:, None] // D_HEADS) > (pg[None, :] // D_HEADS)
    c_later = (same_head & later).astype(np.float32)
    c_same = same_head.astype(np.float32)
    rows = np.arange(g_pages * SUBLANES)
    r_later = (rows[None, :] > rows[:, None]).astype(np.float32)
    r = np.arange(D_HEADS * n_new)
    past_ok = (pg[None, :] % D_HEADS) == (r[:, None] // n_new)
    past_bias = np.where(past_ok, 0.0, NEG_BIG).astype(np.float32)
    cn = np.arange(n_new * D_HEADS)
    new_ok = ((cn[None, :] % D_HEADS) == (r[:, None] // n_new)) & ((cn[None, :] // D_HEADS) <= (r[:, None] % n_new))
    new_bias = np.where(new_ok, 0.0, NEG_BIG).astype(np.float32)
    m_cols = ((cn[:, None] % D_HEADS == cn[None, :] % D_HEADS)
              & (cn[:, None] // D_HEADS <= cn[None, :] // D_HEADS)).astype(np.float32)
    m_rows = ((cn[None, :] % D_HEADS == r[:, None] // n_new)
              & (cn[None, :] // D_HEADS <= r[:, None] % n_new)).astype(np.float32)
    return c_later, c_same, r_later, past_bias, new_bias, m_cols, m_rows


def _page_copies(pt_ref, ck_hbm, cv_hbm, clf_hbm, kbuf, vbuf, lfbuf, sems, step, slot, *,
                 layer_slot, n_groups, g_pages):
    b = step // n_groups
    first_page = (n_groups - 1 - step % n_groups) * g_pages
    copies = []
    for g in range(g_pages):
        page = pt_ref[b, first_page + g]
        copies.append(pltpu.make_async_copy(ck_hbm.at[layer_slot, page], kbuf.at[slot, g], sems.at[0, slot]))
        copies.append(pltpu.make_async_copy(cv_hbm.at[layer_slot, page], vbuf.at[slot, g], sems.at[1, slot]))
        copies.append(pltpu.make_async_copy(clf_hbm.at[layer_slot, page], lfbuf.at[slot, g], sems.at[2, slot]))
    return copies


def _attn_sample_kernel(pt_ref, q_ref, kn_ref, vn_ref, lfr_ref, lfc_ref, dg_ref,
                        cl_ref, cs_ref, rl_ref, pb_ref, nb_ref, mc_ref, mr_ref,
                        ck_hbm, cv_hbm, clf_hbm, o_ref,
                        kbuf, vbuf, lfbuf, sems, m_scr, l_scr, acc_scr, run_scr, base_scr, *,
                        layer_slot, n_groups, g_pages):
    step = pl.program_id(0)
    n_steps = pl.num_programs(0)
    slot = step % 2
    grp = step % n_groups
    scale = LANES ** -0.5
    hi = lax.Precision.HIGHEST
    copies = functools.partial(_page_copies, pt_ref, ck_hbm, cv_hbm, clf_hbm, kbuf, vbuf, lfbuf, sems,
                               layer_slot=layer_slot, n_groups=n_groups, g_pages=g_pages)

    @pl.when(step == 0)
    def _():
        for c in copies(0, 0):
            c.start()

    @pl.when(step + 1 < n_steps)
    def _():
        for c in copies(step + 1, 1 - slot):
            c.start()

    qb = q_ref[0].astype(BF16)
    rowc = jnp.sum(mr_ref[...] * lfr_ref[0], axis=-1, keepdims=True)

    @pl.when(grp == 0)
    def _():
        m_scr[...] = jnp.full(m_scr.shape, NEG_BIG, F32)
        l_scr[...] = jnp.zeros(l_scr.shape, F32)
        acc_scr[...] = jnp.zeros(acc_scr.shape, F32)
        run_scr[...] = jnp.zeros(run_scr.shape, F32)
        base_scr[...] = rowc + pb_ref[...]

    for c in copies(step, slot):
        c.wait()

    n_rows = g_pages * SUBLANES
    lf = lfbuf[slot].reshape(n_rows, LANES)
    within = jnp.dot(lf, cl_ref[...], precision=hi, preferred_element_type=F32)
    rowtot = jnp.dot(lf, cs_ref[...], precision=hi, preferred_element_type=F32)
    later_rows = jnp.dot(rl_ref[...], rowtot, precision=hi, preferred_element_type=F32)
    g_past = (within + later_rows) + run_scr[...]
    run_scr[...] = run_scr[...] + jnp.sum(rowtot, axis=0, keepdims=True)

    pg_rows = PAGE_SIZE * D_HEADS
    n_part = max(g_pages // SAMPLE_SPLIT, 1)
    parts = [(p0, min(p0 + n_part, g_pages)) for p0 in range(0, g_pages, n_part)]
    scores = [_dot_nt(qb, kbuf[slot, p0:p1].reshape((p1 - p0) * pg_rows, LANES).astype(BF16)) * scale
              for p0, p1 in parts]
    base = base_scr[...]
    m_run, l_run = m_scr[...], l_scr[...]
    for (p0, p1), s_part in zip(parts, scores):
        blocks = []
        s_max = None
        for j in range((p1 - p0) * SUBLANES):
            row = p0 * SUBLANES + j
            sj = (s_part[:, j * LANES:(j + 1) * LANES] + base) + g_past[row:row + 1, :]
            blocks.append(sj)
            s_max = sj if s_max is None else jnp.maximum(s_max, sj)
        m_new = jnp.maximum(m_run, jnp.max(s_max, axis=-1, keepdims=True))
        alpha = jnp.exp(m_run - m_new)
        p_sum = None
        p_blocks = []
        for sj in blocks:
            pj = jnp.exp(sj - m_new)
            p_sum = pj if p_sum is None else p_sum + pj
            p_blocks.append(pj.astype(BF16))
        l_run = alpha * l_run + jnp.sum(p_sum, axis=-1, keepdims=True)
        acc_scr[...] = alpha * acc_scr[...] + _dot(
            jnp.concatenate(p_blocks, axis=1),
            vbuf[slot, p0:p1].reshape((p1 - p0) * pg_rows, LANES).astype(BF16))
        m_run = m_new
    m_scr[...] = m_run
    l_scr[...] = l_run

    @pl.when(grp == n_groups - 1)
    def _():
        c_new = jnp.sum(mc_ref[...] * lfc_ref[0], axis=0, keepdims=True)
        s = (_dot_nt(qb, kn_ref[0].astype(BF16)) * scale + rowc) - c_new + nb_ref[...]
        m_old = m_scr[...]
        m_new = jnp.maximum(m_old, jnp.max(s, axis=-1, keepdims=True))
        alpha = jnp.exp(m_old - m_new)
        pr = jnp.exp(s - m_new)
        l_fin = alpha * l_scr[...] + jnp.sum(pr, axis=-1, keepdims=True)
        acc = alpha * acc_scr[...] + _dot(pr.astype(BF16), vn_ref[0].astype(BF16))
        o_ref[0] = (acc / l_fin) * _silu(dg_ref[0])


def _attn_sample(q_hq, kn, vn, lf_row, lf_col, dg_hq, cache_k, cache_v, cache_lf, page_table, *, layer_slot):
    bd, r, dh = q_hq.shape
    n_pages = page_table.shape[1]
    n_new = r // D_HEADS
    n_slots, n_phys = cache_k.shape[:2]
    pg_rows = PAGE_SIZE * D_HEADS
    ck = cache_k.reshape(n_slots, n_phys, pg_rows, dh)
    cv = cache_v.reshape(n_slots, n_phys, pg_rows, dh)
    clf = cache_lf.reshape(n_slots, n_phys, SUBLANES, LANES)
    g_pages = min(PAGES_PER_STEP, n_pages)
    assert n_pages % g_pages == 0
    n_groups = n_pages // g_pages
    tabs = [jnp.asarray(t) for t in _attn_sample_tables(n_new, g_pages)]

    def per_b(shape):
        return pl.BlockSpec((1,) + shape, lambda s, pt: (s // n_groups, 0, 0))

    def const(t):
        return pl.BlockSpec(t.shape, lambda s, pt: (0, 0))

    hbm = pl.BlockSpec(memory_space=pl.ANY)
    grid_spec = pltpu.PrefetchScalarGridSpec(
        num_scalar_prefetch=1,
        grid=(bd * n_groups,),
        in_specs=[per_b((r, dh)), per_b((r, dh)), per_b((r, dh)), per_b((1, r)), per_b((r, 1)), per_b((r, dh))]
        + [const(t) for t in tabs] + [hbm, hbm, hbm],
        out_specs=per_b((r, dh)),
        scratch_shapes=[
            pltpu.VMEM((2, g_pages, pg_rows, dh), F32), pltpu.VMEM((2, g_pages, pg_rows, dh), F32),
            pltpu.VMEM((2, g_pages, SUBLANES, LANES), F32), pltpu.SemaphoreType.DMA((3, 2)),
            pltpu.VMEM((r, 1), F32), pltpu.VMEM((r, 1), F32), pltpu.VMEM((r, dh), F32),
            pltpu.VMEM((1, LANES), F32), pltpu.VMEM((r, LANES), F32)],
    )
    return pl.pallas_call(
        functools.partial(_attn_sample_kernel, layer_slot=layer_slot, n_groups=n_groups, g_pages=g_pages),
        grid_spec=grid_spec,
        out_shape=jax.ShapeDtypeStruct((bd, r, dh), F32),
        compiler_params=_cparams("arbitrary"),
        name="attn_sample",
    )(page_table, q_hq, kn, vn, lf_row, lf_col, dg_hq, *tabs, ck, cv, clf)


TM_NORM = 1024
TM_IN = 2048
TN_IN_EVEN = 512
TN_IN_ODD = 256
TN_IN_SAMPLE = 1024
TM_OUT = 512
TQ_POOL = 512
TQ_ATTN = 256


def kernel(x_prompt, x_sample, c_prompt, c_sample, state_ret, state_pool, cache_k, cache_v, cache_logf,
           page_table, g_pre, g_post, w_ada, b_ada, w_in_even, w_out_even, ln_a_g, ln_a_b, w_s, b_s,
           ret_g, w_in_odd, b_f, w_out_odd, w_pool, pool_scale):
    bp, seq, d = x_prompt.shape
    bd, n_new, _ = x_sample.shape
    depth = g_pre.shape[0]
    n_odd = w_in_odd.shape[0]
    w = d // 2
    n_past = page_table.shape[1] * PAGE_SIZE
    rows_s = bd * n_new

    c_all = jnp.concatenate([c_prompt, c_sample], axis=0)
    c_all = jnp.pad(c_all, ((0, -c_all.shape[0] % (2 * SUBLANES)), (0, 0)))
    mod = _ada_mod(c_all, w_ada, b_ada)

    def mods(l):
        mp = mod[l, :bp].reshape(bp, 1, 3 * d)
        ms = jnp.repeat(mod[l, bp:bp + bd], n_new, axis=0).reshape(1, rows_s, 3 * d)
        return [(m[..., :d], m[..., d:2 * d], m[..., 2 * d:]) for m in (mp, ms)]

    xp = x_prompt.reshape(bp * seq, d)
    xs = x_sample.reshape(rows_s, d)
    ret_p, ret_s, gv_s, pool_p, pool_s, lfp_l, lfs_l = [], [], [], [], [], [], []
    kv_p = kv_s = None
    zeros_state = jnp.zeros((1, bp) + state_ret.shape[2:], F32)
    w_out_even_bf, w_out_odd_bf, w_pool_bf = w_out_even.astype(BF16), w_out_odd.astype(BF16), w_pool.astype(BF16)
    w_in_odd_t = jnp.swapaxes(w_in_odd, 1, 2)
    all_mods = [mods(l) for l in range(depth)]
    (shift_p, scale_p, _), (shift_s, scale_s, _) = all_mods[0]
    g0 = g_pre[0].reshape(1, d)
    hp = _modnorm_call(xp, scale_p, shift_p, g0, per_row=False, rows_per_mod=seq, tm=TM_NORM)
    hs = _modnorm_call(xs, scale_s, shift_s, g0, per_row=True, rows_per_mod=1, tm=rows_s)
    for l in range(depth):
        j = l // 2
        (_, _, gate_p), (_, _, gate_s) = all_mods[l]
        gpost = g_post[l].reshape(1, d)
        if l % 2 == 0:
            w_out = w_out_even_bf
            zp = _inproj_even(hp, w_in_even, layer_slot=j, tm=TM_IN, tn=TN_IN_EVEN)
            zs = _inproj_even(hs, w_in_even, layer_slot=j, tm=rows_s, tn=TN_IN_SAMPLE)
            mix = functools.partial(_even_mixer, ln_g=ln_a_g[j], ln_b=ln_a_b[j], w_s=w_s[j], b_s=b_s[j],
                                    ret_g=ret_g[j])
            ap, rp, sp = mix(zp, zeros_state, s0_slot=0, n_batch=bp, seq=seq, pos0=0, emit_vn=False,
                             out_dtype=BF16)
            as_, rs, ss, vn_s = mix(zs, state_ret, s0_slot=j, n_batch=bd, seq=n_new, pos0=n_past, emit_vn=True,
                                    out_dtype=F32)
            ret_p.append(sp)
            ret_s.append(ss)
            gv_s.append(vn_s.reshape(bd, n_new, w))
        else:
            wf = jnp.pad(w_in_odd_t[j, 6 * w:, :], ((0, LANES - D_HEADS), (0, 0)))
            bf = jnp.pad(b_f[j], (0, LANES - D_HEADS)).reshape(1, LANES)
            w_out = w_out_odd_bf
            wp = w_pool_bf
            zp, kp_all, vp_all, lfp = _inproj_odd(hp, w_in_odd_t, wf, bf, kv_p, layer_slot=j, n_slots=n_odd,
                                                  tm=TM_IN, tn=TN_IN_ODD)
            kv_p = (kp_all, vp_all)
            zs, ks_all, vs_all, lfs = _inproj_odd(hs, w_in_odd_t, wf, bf, kv_s, layer_slot=j, n_slots=n_odd,
                                                  tm=rows_s, tn=TN_IN_SAMPLE)
            kv_s = (ks_all, vs_all)
            ap = _pool_branch(zp, jnp.zeros((bp, POOL_PREV, w), F32), _pool_counts(0, seq), wp, pool_scale[j],
                              layer_slot=j, n_batch=bp, seq=seq, tq=TQ_POOL, out_dtype=BF16)
            fcol, frow = _fcum(lfp, n_batch=bp, seq=seq, blk=TQ_ATTN)
            rp = _attn_prompt(zp, kp_all, vp_all, fcol, frow, layer_slot=j, n_batch=bp, seq=seq, tq=TQ_ATTN,
                              out_dtype=BF16)
            pool_p.append(zp.reshape(bp, seq, 4 * w)[:, seq - POOL_BUF:, :w])
            lfp_l.append(lfp[:, :D_HEADS].reshape(bp, seq, D_HEADS))
            prev = jnp.pad(state_pool[j], ((0, 0), (POOL_PREV - POOL_BUF, 0), (0, 0)))
            as_ = _pool_branch(zs, prev, _pool_counts(n_past, POOL_PREV), wp, pool_scale[j],
                               layer_slot=j, n_batch=bd, seq=n_new, tq=n_new, out_dtype=F32)
            zs4 = zs.reshape(bd, n_new, 4, D_HEADS, LANES)
            to_hq = lambda a: a.transpose(0, 2, 1, 3).reshape(bd, D_HEADS * n_new, LANES)
            lf_new = lfs[:, :D_HEADS].reshape(bd, n_new * D_HEADS)
            o_hq = _attn_sample(to_hq(zs4[:, :, 2]), ks_all[j].reshape(bd, n_new * D_HEADS, LANES),
                                vs_all[j].reshape(bd, n_new * D_HEADS, LANES),
                                lf_new.reshape(bd, 1, -1), lf_new.reshape(bd, -1, 1), to_hq(zs4[:, :, 3]),
                                cache_k, cache_v, cache_logf, page_table, layer_slot=j)
            rs = o_hq.reshape(bd, D_HEADS, n_new, LANES).transpose(0, 2, 1, 3).reshape(rows_s, w)
            c_in_s = zs[:, :w].reshape(bd, n_new, w)
            pool_s.append(jnp.concatenate([state_pool[j], c_in_s], axis=1)[:, -POOL_BUF:])
            lfs_l.append(lfs[:, :D_HEADS].reshape(bd, n_new, D_HEADS))
        next_p = next_s = None
        if l + 1 < depth:
            (shift_p, scale_p, _), (shift_s, scale_s, _) = all_mods[l + 1]
            g_next = g_pre[l + 1].reshape(1, d)
            next_p, next_s = (scale_p, shift_p, g_next), (scale_s, shift_s, g_next)
        xp, hp = _outproj(ap, rp, xp, gate_p, gpost, w_out, next_p, layer_slot=j, per_row=False,
                          rows_per_mod=seq, tm=TM_OUT)
        xs, hs = _outproj(as_, rs, xs, gate_s, gpost, w_out, next_s, layer_slot=j, per_row=True,
                          rows_per_mod=1, tm=rows_s)

    dh = LANES
    return (xp.reshape(bp, seq, d), xs.reshape(bd, n_new, d),
            jnp.stack(ret_p), jnp.stack(ret_s), jnp.stack(gv_s), jnp.stack(pool_p), jnp.stack(pool_s),
            kv_p[0].reshape(n_odd, bp, seq, D_HEADS, dh), kv_p[1].reshape(n_odd, bp, seq, D_HEADS, dh),
            jnp.stack(lfp_l),
            kv_s[0].reshape(n_odd, bd, n_new, D_HEADS, dh), kv_s[1].reshape(n_odd, bd, n_new, D_HEADS, dh),
            jnp.stack(lfs_l))
```

```python
import functools

import numpy as np
import jax
import jax.numpy as jnp
from jax import lax
from jax.experimental import pallas as pl
from jax.experimental.pallas import tpu as pltpu

F32 = jnp.float32
BF16 = jnp.bfloat16

EPS = 1e-6
ROPE_BASE = 10000.0
CHUNK = 128
A_GROUPS = 8
B_HEADS = 4
POOL_WINDOWS = (2, 4, 8, 16)
POOL_BUF = 15
D_HEADS = 8
PAGE_SIZE = 128
LANES = 128
SUBLANES = 8
NEG_BIG = -1e30
VMEM_LIMIT = 56 * 1024 * 1024


def _cparams(*sem):
    return pltpu.CompilerParams(dimension_semantics=sem, vmem_limit_bytes=VMEM_LIMIT)


def _silu(x):
    return x * jax.nn.sigmoid(x)


def _log_sigmoid(x):
    return jnp.minimum(x, 0.0) - jnp.log1p(jnp.exp(-jnp.abs(x)))


def _dot(a, b):
    return jnp.dot(a, b, preferred_element_type=F32)


def _dot_nt(a, b):
    return lax.dot_general(a, b, (((1,), (1,)), ((), ())), preferred_element_type=F32)


def _dot_tn(a, b):
    return lax.dot_general(a, b, (((0,), (0,)), ((), ())), preferred_element_type=F32)


def _ada_kernel(c_ref, w_ref, b_ref, o_ref):
    a = _silu(c_ref[...]).astype(BF16)
    o_ref[0] = _dot(a, w_ref[0].astype(BF16)) + b_ref[0]


def _ada_mod(c_all, w_ada, b_ada, tn=1024):
    depth, d, n = w_ada.shape
    r = c_all.shape[0]
    return pl.pallas_call(
        _ada_kernel,
        grid=(depth, n // tn),
        in_specs=[
            pl.BlockSpec((r, d), lambda l, j: (0, 0)),
            pl.BlockSpec((1, d, tn), lambda l, j: (l, 0, j)),
            pl.BlockSpec((1, 1, tn), lambda l, j: (l, 0, j)),
        ],
        out_specs=pl.BlockSpec((1, r, tn), lambda l, j: (l, 0, j)),
        out_shape=jax.ShapeDtypeStruct((depth, r, n), F32),
        compiler_params=_cparams("arbitrary", "arbitrary"),
        name="ada_mod",
    )(c_all, w_ada, b_ada.reshape(depth, 1, n))


def _modnorm(x, scale, shift, g):
    y = x * lax.rsqrt(jnp.mean(x * x, axis=-1, keepdims=True) + EPS) * g
    return (y * (1.0 + scale) + shift).astype(BF16)


def _modnorm_kernel(x_ref, sc_ref, sh_ref, g_ref, h_ref):
    h_ref[...] = _modnorm(x_ref[...], sc_ref[0], sh_ref[0], g_ref[...])


def _mod_spec(per_row, rows_per_mod, tm, d):
    if per_row:
        return pl.BlockSpec((1, tm, d), lambda i: (0, i, 0))
    return pl.BlockSpec((1, 1, d), lambda i: ((i * tm) // rows_per_mod, 0, 0))


def _modnorm_call(x2d, scale, shift, g, *, per_row, rows_per_mod, tm):
    rows, d = x2d.shape
    return pl.pallas_call(
        _modnorm_kernel,
        grid=(rows // tm,),
        in_specs=[pl.BlockSpec((tm, d), lambda i: (i, 0)),
                  _mod_spec(per_row, rows_per_mod, tm, d), _mod_spec(per_row, rows_per_mod, tm, d),
                  pl.BlockSpec((1, d), lambda i: (0, 0))],
        out_specs=pl.BlockSpec((tm, d), lambda i: (i, 0)),
        out_shape=jax.ShapeDtypeStruct((rows, d), BF16),
        compiler_params=_cparams("arbitrary"),
        name="modnorm",
    )(x2d, scale, shift, g)


def _inproj_even_kernel(h_ref, hs_ref, w_ref, z_ref, zs_ref):
    wb = w_ref[0].astype(BF16)
    z_ref[...] = _dot(h_ref[...], wb)

    @pl.when(pl.program_id(0) == 0)
    def _():
        zs_ref[...] = _dot(hs_ref[...], wb)


def _inproj_odd_kernel(h_ref, hs_ref, w_ref, wf_ref, bf_ref, *rest, tiles_per_group, kv_slot):
    z_ref, k_ref, v_ref, lf_ref, zs_ref, ks_ref, vs_ref, lfs_ref = rest[-8:]
    j = pl.program_id(1)
    group = j // tiles_per_group

    def emit(x_ref, z_o, k_o, v_o, lf_o):
        def project():
            return _dot_nt(x_ref[...], w_ref[0].astype(BF16))

        def put(kv_ref):
            for s in range(kv_ref.shape[0]):
                kv_ref[s] = project() if s == kv_slot else jnp.zeros(kv_ref.shape[1:], F32)

        @pl.when(j == 0)
        def _():
            lf_o[...] = _log_sigmoid(_dot_nt(x_ref[...], wf_ref[...].astype(BF16)) + bf_ref[...])

        @pl.when(jnp.logical_or(group < 3, group == 5))
        def _():
            z_o[...] = project()

        @pl.when(group == 3)
        def _():
            put(k_o)

        @pl.when(group == 4)
        def _():
            put(v_o)

    emit(h_ref, z_ref, k_ref, v_ref, lf_ref)

    @pl.when(pl.program_id(0) == 0)
    def _():
        emit(hs_ref, zs_ref, ks_ref, vs_ref, lfs_ref)


def _inproj_even(h, hs, w_all, *, layer_slot, tm, tn):
    rows, d = h.shape
    rows_s = hs.shape[0]
    n = w_all.shape[2]
    n_j = n // tn
    return pl.pallas_call(
        _inproj_even_kernel,
        grid=(rows // tm, n_j),
        in_specs=[
            pl.BlockSpec((tm, d), lambda i, j: (i, 0)),
            pl.BlockSpec((rows_s, d), lambda i, j: (0, 0)),
            pl.BlockSpec((1, d, tn), lambda i, j: (layer_slot, 0, j)),
        ],
        out_specs=[pl.BlockSpec((tm, tn), lambda i, j: (i, j)),
                   pl.BlockSpec((rows_s, tn), lambda i, j: (0, jnp.where(i == 0, j, n_j - 1)))],
        out_shape=[jax.ShapeDtypeStruct((rows, n), F32), jax.ShapeDtypeStruct((rows_s, n), F32)],
        compiler_params=_cparams("arbitrary", "arbitrary"),
        name="inproj_even",
    )(h, hs, w_all)


def _inproj_odd(h, hs, w_all_t, wf, bf, kv_prev, *, layer_slot, n_slots, tm, tn):
    rows, d = h.shape
    rows_s = hs.shape[0]
    w = (w_all_t.shape[1] - D_HEADS) // 6
    tpg = w // tn
    n_tiles = 6 * tpg

    creates_kv = kv_prev is None
    kv_slots = n_slots if creates_kv else 1
    kv_first = 0 if creates_kv else layer_slot

    def zcol(j):
        return j - jnp.clip(j - (3 * tpg - 1), 0, 2 * tpg)

    def kcol(j):
        return jnp.clip(j - 3 * tpg, 0, tpg - 1)

    def vcol(j):
        return jnp.clip(j - 4 * tpg, 0, tpg - 1)

    def sample_j(i, j):
        return jnp.where(i == 0, j, n_tiles - 1)

    in_specs = [
        pl.BlockSpec((tm, d), lambda i, j: (i, 0)),
        pl.BlockSpec((rows_s, d), lambda i, j: (0, 0)),
        pl.BlockSpec((1, tn, d), lambda i, j: (layer_slot, j, 0)),
        pl.BlockSpec((LANES, d), lambda i, j: (0, 0)),
        pl.BlockSpec((1, LANES), lambda i, j: (0, 0)),
    ]
    args = [h, hs, w_all_t, wf, bf]
    aliases = {}
    if kv_prev is not None:
        in_specs += [pl.BlockSpec(memory_space=pl.ANY)] * 4
        aliases = {len(args): 1, len(args) + 1: 2, len(args) + 2: 5, len(args) + 3: 6}
        args += list(kv_prev)
    return pl.pallas_call(
        functools.partial(_inproj_odd_kernel, tiles_per_group=tpg, kv_slot=layer_slot if creates_kv else 0),
        grid=(rows // tm, n_tiles),
        in_specs=in_specs,
        out_specs=[
            pl.BlockSpec((tm, tn), lambda i, j: (i, zcol(j))),
            pl.BlockSpec((kv_slots, tm, tn), lambda i, j: (kv_first, i, kcol(j))),
            pl.BlockSpec((kv_slots, tm, tn), lambda i, j: (kv_first, i, vcol(j))),
            pl.BlockSpec((tm, LANES), lambda i, j: (i, 0)),
            pl.BlockSpec((rows_s, tn), lambda i, j: (0, zcol(sample_j(i, j)))),
            pl.BlockSpec((kv_slots, rows_s, tn), lambda i, j: (kv_first, 0, kcol(sample_j(i, j)))),
            pl.BlockSpec((kv_slots, rows_s, tn), lambda i, j: (kv_first, 0, vcol(sample_j(i, j)))),
            pl.BlockSpec((rows_s, LANES), lambda i, j: (0, 0)),
        ],
        out_shape=[
            jax.ShapeDtypeStruct((rows, 4 * w), F32),
            jax.ShapeDtypeStruct((n_slots, rows, w), F32),
            jax.ShapeDtypeStruct((n_slots, rows, w), F32),
            jax.ShapeDtypeStruct((rows, LANES), F32),
            jax.ShapeDtypeStruct((rows_s, 4 * w), F32),
            jax.ShapeDtypeStruct((n_slots, rows_s, w), F32),
            jax.ShapeDtypeStruct((n_slots, rows_s, w), F32),
            jax.ShapeDtypeStruct((rows_s, LANES), F32),
        ],
        input_output_aliases=aliases,
        compiler_params=_cparams("arbitrary", "arbitrary"),
        name="inproj_odd",
    )(*args)


def _outproj_kernel(a_ref, b_ref, x_ref, gate_ref, as_ref, bs_ref, xs_ref, gates_ref, g_ref, w1_ref, w2_ref,
                    *rest, with_next):
    if with_next:
        sc_ref, sh_ref, scs_ref, shs_ref, gn_ref, o_ref, h_ref, os_ref, hs_ref = rest
    else:
        o_ref, os_ref = rest

    def finish(a_r, b_r, x_r, gate_r, o_r, next_refs):
        y = _dot(a_r[...].astype(BF16), w1_ref[0]) + _dot(b_r[...].astype(BF16), w2_ref[0])
        yn = y * lax.rsqrt(jnp.mean(y * y, axis=-1, keepdims=True) + EPS) * g_ref[...]
        x_new = x_r[...] + gate_r[0] * yn
        if next_refs is not None:
            sc_r, sh_r, h_r = next_refs
            h_r[...] = _modnorm(x_new, sc_r[0], sh_r[0], gn_ref[...])
        o_r[...] = x_new

    finish(a_ref, b_ref, x_ref, gate_ref, o_ref, (sc_ref, sh_ref, h_ref) if with_next else None)

    @pl.when(pl.program_id(0) == 0)
    def _():
        finish(as_ref, bs_ref, xs_ref, gates_ref, os_ref, (scs_ref, shs_ref, hs_ref) if with_next else None)


def _outproj(prompt, sample, g, w_bf, next_mod, *, layer_slot, rows_per_mod, tm):
    a, b, x2d, gate = prompt
    a_s, b_s, xs2d, gate_s = sample
    rows, d = x2d.shape
    rows_s = xs2d.shape[0]
    w = a.shape[1]
    mod_spec = _mod_spec(False, rows_per_mod, tm, d)
    mod_s_spec = pl.BlockSpec((1, rows_s, d), lambda i: (0, 0, 0))
    row_spec = pl.BlockSpec((tm, d), lambda i: (i, 0))
    row_s_spec = pl.BlockSpec((rows_s, d), lambda i: (0, 0))
    half_s_spec = pl.BlockSpec((rows_s, w), lambda i: (0, 0))
    vec_spec = pl.BlockSpec((1, d), lambda i: (0, 0))
    in_specs = [
        pl.BlockSpec((tm, w), lambda i: (i, 0)),
        pl.BlockSpec((tm, w), lambda i: (i, 0)),
        row_spec, mod_spec,
        half_s_spec, half_s_spec, row_s_spec, mod_s_spec,
        vec_spec,
        pl.BlockSpec((1, w, d), lambda i: (layer_slot, 0, 0)),
        pl.BlockSpec((1, w, d), lambda i: (layer_slot, 1, 0)),
    ]
    args = [a, b, x2d, gate, a_s, b_s, xs2d, gate_s, g, w_bf, w_bf]
    out_specs = [row_spec]
    out_shape = [jax.ShapeDtypeStruct((rows, d), F32)]
    if next_mod is not None:
        (sc_p, sh_p), (sc_s, sh_s), g_next = next_mod
        in_specs += [mod_spec, mod_spec, mod_s_spec, mod_s_spec, vec_spec]
        args += [sc_p, sh_p, sc_s, sh_s, g_next]
        out_specs.append(row_spec)
        out_shape.append(jax.ShapeDtypeStruct((rows, d), BF16))
    out_specs.append(row_s_spec)
    out_shape.append(jax.ShapeDtypeStruct((rows_s, d), F32))
    if next_mod is not None:
        out_specs.append(row_s_spec)
        out_shape.append(jax.ShapeDtypeStruct((rows_s, d), BF16))
    res = pl.pallas_call(
        functools.partial(_outproj_kernel, with_next=next_mod is not None),
        grid=(rows // tm,),
        in_specs=in_specs,
        out_specs=out_specs,
        out_shape=out_shape,
        compiler_params=_cparams("arbitrary"),
        name="outproj",
    )(*args)
    if next_mod is not None:
        xp, hp, xs, hs = res
        return xp, hp, xs, hs
    xp, xs = res
    return xp, None, xs, None


def _retention_tables(c_len, pos0, n_rows):
    lg = np.log(1.0 - 2.0 ** (-5.0 - np.arange(B_HEADS, dtype=np.float64)))
    t = np.arange(CHUNK, dtype=np.float64)
    diff = t[:, None] - t[None, :]
    dmask = np.where(diff >= 0, np.exp(lg[:, None, None] * np.maximum(diff, 0.0)), 0.0).astype(np.float32)
    qdec = np.exp(lg[None, :] * (t + 1.0)[:, None]).astype(np.float32)
    kdec = np.exp(lg[None, :] * np.maximum(c_len - 1.0 - t, 0.0)[:, None]).astype(np.float32)
    cdec = tuple(float(v) for v in np.exp(lg * c_len).astype(np.float32))
    half = LANES
    inv = ROPE_BASE ** (-np.arange(half, dtype=np.float64) / half)
    pos = (pos0 + np.arange(n_rows)).astype(np.float64)
    ang = pos[:, None] * inv[None, :]
    return dmask, qdec, kdec, cdec, ang


MIXER_CHUNKS_PER_STEP = 4


def _even_mixer_kernel(u_ref, v_ref, ga_ref, q_ref, k_ref, vv_ref, gr_ref,
                       lng_ref, lnb_ref, ws_ref, bst_ref, retg_ref, cos_ref, sin_ref,
                       dmask_ref, qdec_ref, kdec_ref, s0_ref, *rest, c_in, cdec, emit_vn):
    if emit_vn:
        ao_ref, ro_ref, so_ref, vn_ref, s_scr = rest
    else:
        ao_ref, ro_ref, so_ref, s_scr = rest
    c = pl.program_id(1)

    @pl.when(c == 0)
    def _():
        s_scr[...] = s0_ref[0, 0]

    n_sub = max(u_ref.shape[0] // CHUNK, 1)
    w = u_ref.shape[1]
    gw = w // A_GROUPS
    dk = w // B_HEADS
    half = dk // 2
    causal = (lax.broadcasted_iota(jnp.int32, (CHUNK, CHUNK), 0)
              >= lax.broadcasted_iota(jnp.int32, (CHUNK, CHUNK), 1))
    wms = [jnp.where(causal, ws_ref[g], 0.0).astype(BF16) for g in range(A_GROUPS)]
    for cc in range(n_sub):
        r0 = cc * CHUNK
        n_out = min(c_in, CHUNK)
        rows = slice(r0, r0 + n_out)

        def ld(ref):
            if c_in >= CHUNK:
                return ref[r0:r0 + CHUNK, :]
            return jnp.concatenate([ref[...], jnp.zeros((CHUNK - c_in, w), F32)], axis=0)

        v = ld(v_ref)
        xc = v - jnp.mean(v, axis=-1, keepdims=True)
        vn = xc * lax.rsqrt(jnp.mean(xc * xc, axis=-1, keepdims=True) + EPS) * lng_ref[...] + lnb_ref[...]
        if emit_vn:
            vn_ref[rows, :] = vn[:n_out]
        vnb = vn.astype(BF16)
        u = ld(u_ref)
        ga = ld(ga_ref)
        for g in range(A_GROUPS):
            sl = slice(g * gw, (g + 1) * gw)
            mixed = _dot(wms[g], vnb[:, sl]) + bst_ref[:, g:g + 1]
            ao_ref[rows, sl] = (u[:, sl] * mixed * _silu(ga[:, sl]))[:n_out].astype(ao_ref.dtype)

        cos = cos_ref[r0:r0 + CHUNK, :]
        sin = sin_ref[r0:r0 + CHUNK, :]
        q = ld(q_ref)
        k = ld(k_ref)
        vv = ld(vv_ref)
        gr = ld(gr_ref)
        for h in range(B_HEADS):
            sl = slice(h * dk, (h + 1) * dk)
            q1, q2 = q[:, h * dk:h * dk + half], q[:, h * dk + half:(h + 1) * dk]
            k1, k2 = k[:, h * dk:h * dk + half], k[:, h * dk + half:(h + 1) * dk]
            qr = jnp.concatenate([q1 * cos - q2 * sin, q1 * sin + q2 * cos], axis=1)
            kr = jnp.concatenate([k1 * cos - k2 * sin, k1 * sin + k2 * cos], axis=1) * (dk ** -0.5)
            qrb = qr.astype(BF16)
            vb = vv[:, sl].astype(BF16)
            inner = _dot_nt(qrb, kr.astype(BF16)) * dmask_ref[h]
            s = s_scr[h]
            o = _dot(inner.astype(BF16), vb) + _dot(qrb, s.astype(BF16)) * qdec_ref[:, h:h + 1]
            kd = (kr * kdec_ref[:, h:h + 1]).astype(BF16)
            s_scr[h] = s * cdec[h] + _dot_tn(kd, vb)
            on = o * lax.rsqrt(jnp.mean(o * o, axis=-1, keepdims=True) + EPS) * retg_ref[:, sl]
            ro_ref[rows, sl] = (on * _silu(gr[:, sl]))[:n_out].astype(ro_ref.dtype)

    @pl.when(c == pl.num_programs(1) - 1)
    def _():
        so_ref[0] = s_scr[...]


def _even_mixer(z, s0_all, ln_g, ln_b, w_s, b_s, ret_g, *, s0_slot, n_batch, seq, pos0, emit_vn, out_dtype):
    rows, n = z.shape
    s_shape = s0_all.shape[2:]
    w = n // 7
    c_len = CHUNK if seq % CHUNK == 0 else seq
    n_chunks_total = seq // min(seq, CHUNK)
    per_step = min(MIXER_CHUNKS_PER_STEP, n_chunks_total)
    assert n_chunks_total % per_step == 0
    c_in = min(seq, CHUNK) * per_step
    n_chunks = n_chunks_total // per_step
    dmask, qdec, kdec, cdec, ang = _retention_tables(c_len, pos0, n_chunks_total * CHUNK)
    cos = jnp.asarray(np.cos(ang).astype(np.float32))
    sin = jnp.asarray(np.sin(ang).astype(np.float32))

    def zspec(col):
        return pl.BlockSpec((c_in, w), lambda b, c, col=col: (b * n_chunks + c, col))

    def const(shape):
        return pl.BlockSpec(shape, lambda b, c: (0,) * len(shape))

    row_spec = pl.BlockSpec((c_in, w), lambda b, c: (b * n_chunks + c, 0))
    state_in_spec = pl.BlockSpec((1, 1) + s_shape, lambda b, c: (s0_slot, b, 0, 0, 0))
    state_spec = pl.BlockSpec((1,) + s_shape, lambda b, c: (b, 0, 0, 0))
    out_specs = [row_spec, row_spec, state_spec]
    out_shape = [jax.ShapeDtypeStruct((rows, w), out_dtype),
                 jax.ShapeDtypeStruct((rows, w), out_dtype),
                 jax.ShapeDtypeStruct((n_batch,) + s_shape, F32)]
    if emit_vn:
        out_specs.append(row_spec)
        out_shape.append(jax.ShapeDtypeStruct((rows, w), F32))
    return pl.pallas_call(
        functools.partial(_even_mixer_kernel, c_in=c_in, cdec=cdec, emit_vn=emit_vn),
        grid=(n_batch, n_chunks),
        in_specs=[zspec(i) for i in range(7)] + [
            const((1, w)), const((1, w)), const(w_s.shape), const((CHUNK, A_GROUPS)), const((1, w)),
            pl.BlockSpec((per_step * CHUNK, LANES), lambda b, c: (c, 0)),
            pl.BlockSpec((per_step * CHUNK, LANES), lambda b, c: (c, 0)),
            const(dmask.shape), const(qdec.shape), const(kdec.shape),
            state_in_spec,
        ],
        out_specs=out_specs,
        out_shape=out_shape,
        scratch_shapes=[pltpu.VMEM(s_shape, F32)],
        compiler_params=_cparams("arbitrary", "arbitrary"),
        name="even_mixer",
    )(z, z, z, z, z, z, z, ln_g.reshape(1, w), ln_b.reshape(1, w), w_s, b_s.T, ret_g.reshape(1, w),
      cos, sin, jnp.asarray(dmask), jnp.asarray(qdec), jnp.asarray(kdec), s0_all)


POOL_PREV = 16
POOL_LEAD = SUBLANES


def _pool_kernel(cin_ref, cg_ref, prev_ref, cnt_ref, wp_ref, ps_ref, o_ref, e0, e1, e2, e3, e4, *, tq, t_pad):
    t = pl.program_id(1)
    base = POOL_LEAD + POOL_PREV
    total = base + t_pad
    w = cin_ref.shape[1]
    gw = w // len(POOL_WINDOWS)

    @pl.when(t == 0)
    def _():
        for e in (e0, e1, e2, e3, e4):
            e[0:POOL_LEAD, :] = jnp.zeros((POOL_LEAD, w), F32)
        e0[POOL_LEAD:base, :] = prev_ref[0]
        if t_pad > tq:
            e0[base + tq:total, :] = jnp.zeros((t_pad - tq, w), F32)

    e0[base:base + tq, :] = cin_ref[...]
    levels = (e0, e1, e2, e3, e4)
    for lvl in range(1, 5):
        shift = 1 << (lvl - 1)
        c0 = (lvl - 1) * gw
        src, dst = levels[lvl - 1], levels[lvl]
        dst[POOL_LEAD:total, c0:] = (src[POOL_LEAD:total, c0:]
                                     + src[POOL_LEAD - shift:total - shift, c0:])
    cur = e0[base:total, :]
    cg = cg_ref[...]
    for gi in range(len(POOL_WINDOWS)):
        sl = slice(gi * gw, (gi + 1) * gw)
        win = levels[gi + 1][base:total, sl]
        pooled = win / cnt_ref[:, gi:gi + 1] - cur[:, sl]
        mixed = _dot(pooled.astype(BF16), wp_ref[0, gi]) * ps_ref[:, sl]
        o_ref[:, sl] = (mixed[:tq] * _silu(cg[:, sl])).astype(o_ref.dtype)
    e0[POOL_LEAD:base, :] = e0[POOL_LEAD + t_pad:base + t_pad, :]


def _pool_branch(z_main, prev, cnt, wp_bf, pool_scale, *, layer_slot, n_batch, seq, tq, out_dtype):
    rows = z_main.shape[0]
    w = wp_bf.shape[1] * wp_bf.shape[2]
    t_pad = max(tq, POOL_PREV)
    n_t = seq // tq
    ext = pltpu.VMEM((POOL_LEAD + POOL_PREV + t_pad, w), F32)
    return pl.pallas_call(
        functools.partial(_pool_kernel, tq=tq, t_pad=t_pad),
        grid=(n_batch, n_t),
        in_specs=[
            pl.BlockSpec((tq, w), lambda b, t: (b * n_t + t, 0)),
            pl.BlockSpec((tq, w), lambda b, t: (b * n_t + t, 1)),
            pl.BlockSpec((1, POOL_PREV, w), lambda b, t: (b, 0, 0)),
            pl.BlockSpec((t_pad, len(POOL_WINDOWS)), lambda b, t: (t, 0)),
            pl.BlockSpec((1,) + wp_bf.shape[1:], lambda b, t: (layer_slot, 0, 0, 0)),
            pl.BlockSpec((1, w), lambda b, t: (0, 0)),
        ],
        out_specs=pl.BlockSpec((tq, w), lambda b, t: (b * n_t + t, 0)),
        out_shape=jax.ShapeDtypeStruct((rows, w), out_dtype),
        scratch_shapes=[ext] * 5,
        compiler_params=_cparams("arbitrary", "arbitrary"),
        name="pool_branch",
    )(z_main, z_main, prev, cnt, wp_bf, pool_scale.reshape(1, w))


def _pool_counts(pos0, n_rows):
    pos = pos0 + np.arange(n_rows)
    return jnp.asarray(np.stack([np.minimum(pos + 1, wd) for wd in POOL_WINDOWS], axis=1).astype(np.float32))


def _split3(x):
    a = x.astype(BF16)
    r1 = x - a.astype(F32)
    b = r1.astype(BF16)
    c = (r1 - b.astype(F32)).astype(BF16)
    return a, b, c


def _fcum_kernel(lf_ref, fc_ref, fr_ref, *, blk):
    n_blk = lf_ref.shape[0] // blk
    tri = (lax.broadcasted_iota(jnp.int32, (blk, blk), 0)
           >= lax.broadcasted_iota(jnp.int32, (blk, blk), 1)).astype(BF16)
    carry = jnp.zeros((1, LANES), F32)
    for i in range(n_blk):
        a, b, c = _split3(lf_ref[i * blk:(i + 1) * blk, :])
        cs = (_dot(tri, a) + _dot(tri, b)) + _dot(tri, c) + carry
        carry = cs[blk - 1:blk, :]
        fc_ref[i * blk:(i + 1) * blk, :] = cs
        fr_ref[0, i] = cs.T[:D_HEADS, :]


def _fcum(lf, *, n_batch, seq, blk):
    return pl.pallas_call(
        functools.partial(_fcum_kernel, blk=blk),
        grid=(n_batch,),
        in_specs=[pl.BlockSpec((seq, LANES), lambda b: (b, 0))],
        out_specs=[pl.BlockSpec((seq, LANES), lambda b: (b, 0)),
                   pl.BlockSpec((1, seq // blk, D_HEADS, blk), lambda b: (b, 0, 0, 0))],
        out_shape=[jax.ShapeDtypeStruct((n_batch * seq, LANES), F32),
                   jax.ShapeDtypeStruct((n_batch, seq // blk, D_HEADS, blk), F32)],
        compiler_params=_cparams("arbitrary"),
        name="forget_cumsum",
    )(lf)


SCORE_LEAD = 3


def _attn_prompt_kernel(q_ref, dg_ref, k_ref, v_ref, fc_ref, fr_ref, o_ref,
                        kb_scr, vt_scr, qt_scr, *acc_scrs, tq):
    i = pl.program_id(1)
    dh = LANES
    scale = dh ** -0.5
    n_blk = k_ref.shape[1] // tq

    @pl.when(i == 0)
    def _():
        kb_scr[...] = k_ref[0].astype(BF16)
        for h in range(D_HEADS):
            for kb in range(n_blk):
                vt_scr[kb, h * dh:(h + 1) * dh, :] = (
                    v_ref[0, kb * tq:(kb + 1) * tq, h * dh:(h + 1) * dh].T.astype(BF16))

    for h in range(D_HEADS):
        qt_scr[h] = q_ref[:, h * dh:(h + 1) * dh].T.astype(BF16)
        acc_scrs[h][...] = jnp.zeros((dh, tq), F32)
    key_le_query = (lax.broadcasted_iota(jnp.int32, (tq, tq), 0)
                    <= lax.broadcasted_iota(jnp.int32, (tq, tq), 1))

    def block(kb, stats, masked):
        r0 = pl.multiple_of(kb * tq, tq)
        new_stats = []

        def scores(h):
            sl = slice(h * dh, (h + 1) * dh)
            return _dot(kb_scr[pl.ds(r0, tq), sl], qt_scr[h]) * scale - fc_ref[pl.ds(r0, tq), h:h + 1]

        pending = [scores(h) for h in range(SCORE_LEAD)]
        for h in range(D_HEADS):
            sl = slice(h * dh, (h + 1) * dh)
            m_old, l_old = stats[h]
            fq = fr_ref[0, 0, h:h + 1, :]
            u = pending.pop(0)
            if h + SCORE_LEAD < D_HEADS:
                pending.append(scores(h + SCORE_LEAD))
            if masked:
                u = jnp.where(key_le_query, u, NEG_BIG)
            m_new = jnp.maximum(m_old, jnp.max(u, axis=0, keepdims=True) + fq)
            alpha = jnp.exp(m_old - m_new)
            p = jnp.exp(u - (m_new - fq))
            new_stats.append((m_new, alpha * l_old + jnp.sum(p, axis=0, keepdims=True)))
            acc_scrs[h][...] = alpha * acc_scrs[h][...] + _dot(vt_scr[kb, sl, :], p.astype(BF16))
        return tuple(new_stats)

    stats0 = tuple((jnp.full((1, tq), NEG_BIG, F32), jnp.zeros((1, tq), F32)) for _ in range(D_HEADS))
    stats = lax.fori_loop(0, i, lambda kb, st: block(kb, st, False), stats0)
    stats = block(i, stats, True)
    for h in range(D_HEADS):
        sl = slice(h * dh, (h + 1) * dh)
        o_ref[:, sl] = ((acc_scrs[h][...] / stats[h][1]).T * _silu(dg_ref[:, sl])).astype(o_ref.dtype)


def _attn_prompt(z_main, k_all, v_all, fcol, frow, *, layer_slot, n_batch, seq, tq, out_dtype):
    rows = z_main.shape[0]
    w = k_all.shape[2]
    n_q = seq // tq
    return pl.pallas_call(
        functools.partial(_attn_prompt_kernel, tq=tq),
        grid=(n_batch, n_q),
        in_specs=[
            pl.BlockSpec((tq, w), lambda b, i: (b * n_q + i, 2)),
            pl.BlockSpec((tq, w), lambda b, i: (b * n_q + i, 3)),
            pl.BlockSpec((1, seq, w), lambda b, i: (layer_slot, b, 0)),
            pl.BlockSpec((1, seq, w), lambda b, i: (layer_slot, b, 0)),
            pl.BlockSpec((seq, LANES), lambda b, i: (b, 0)),
            pl.BlockSpec((1, 1, D_HEADS, tq), lambda b, i: (b, i, 0, 0)),
        ],
        out_specs=pl.BlockSpec((tq, w), lambda b, i: (b * n_q + i, 0)),
        out_shape=jax.ShapeDtypeStruct((rows, w), out_dtype),
        scratch_shapes=[pltpu.VMEM((seq, w), BF16), pltpu.VMEM((n_q, w, tq), BF16),
                        pltpu.VMEM((D_HEADS, LANES, tq), BF16)]
        + [pltpu.VMEM((LANES, tq), F32)] * D_HEADS,
        compiler_params=_cparams("arbitrary", "arbitrary"),
        name="attn_prompt",
    )(z_main, z_main, k_all, v_all, fcol, frow)


PAGES_PER_STEP = 16
SAMPLE_SPLIT = 4


def _attn_sample_tables(n_new, g_pages):
    pg = np.arange(LANES)
    same_head = (pg[:, None] % D_HEADS) == (pg[None, :] % D_HEADS)
    later = (pg[:, None] // D_HEADS) > (pg[None, :] // D_HEADS)
    c_later = (same_head & later).astype(np.float32)
    c_same = same_head.astype(np.float32)
    rows = np.arange(g_pages * SUBLANES)
    r_later = (rows[None, :] > rows[:, None]).astype(np.float32)
    r = np.arange(D_HEADS * n_new)
    past_ok = (pg[None, :] % D_HEADS) == (r[:, None] // n_new)
    past_bias = np.where(past_ok, 0.0, NEG_BIG).astype(np.float32)
    cn = np.arange(n_new * D_HEADS)
    new_ok = ((cn[None, :] % D_HEADS) == (r[:, None] // n_new)) & ((cn[None, :] // D_HEADS) <= (r[:, None] % n_new))
    new_bias = np.where(new_ok, 0.0, NEG_BIG).astype(np.float32)
    m_cols = ((cn[:, None] % D_HEADS == cn[None, :] % D_HEADS)
              & (cn[:, None] // D_HEADS <= cn[None, :] // D_HEADS)).astype(np.float32)
    m_rows = ((cn[None, :] % D_HEADS == r[:, None] // n_new)
              & (cn[None, :] // D_HEADS <= r[:, None] % n_new)).astype(np.float32)
    return c_later, c_same, r_later, past_bias, new_bias, m_cols, m_rows


def _page_copies(pt_ref, ck_hbm, cv_hbm, clf_hbm, kbuf, vbuf, lfbuf, sems, step, slot, *,
                 layer_slot, n_groups, g_pages):
    b = step // n_groups
    first_page = (n_groups - 1 - step % n_groups) * g_pages
    copies = []
    for g in range(g_pages):
        page = pt_ref[b, first_page + g]
        copies.append(pltpu.make_async_copy(ck_hbm.at[layer_slot, page], kbuf.at[slot, g], sems.at[0, slot]))
        copies.append(pltpu.make_async_copy(cv_hbm.at[layer_slot, page], vbuf.at[slot, g], sems.at[1, slot]))
        copies.append(pltpu.make_async_copy(clf_hbm.at[layer_slot, page], lfbuf.at[slot, g], sems.at[2, slot]))
    return copies


def _attn_sample_kernel(pt_ref, q_ref, kn_ref, vn_ref, lfr_ref, lfc_ref, dg_ref,
                        cl_ref, cs_ref, rl_ref, pb_ref, nb_ref, mc_ref, mr_ref,
                        ck_hbm, cv_hbm, clf_hbm, o_ref,
                        kbuf, vbuf, lfbuf, sems, m_scr, l_scr, acc_scr, run_scr, base_scr, *,
                        layer_slot, n_groups, g_pages):
    step = pl.program_id(0)
    n_steps = pl.num_programs(0)
    slot = step % 2
    grp = step % n_groups
    scale = LANES ** -0.5
    hi = lax.Precision.HIGHEST
    copies = functools.partial(_page_copies, pt_ref, ck_hbm, cv_hbm, clf_hbm, kbuf, vbuf, lfbuf, sems,
                               layer_slot=layer_slot, n_groups=n_groups, g_pages=g_pages)

    @pl.when(step == 0)
    def _():
        for c in copies(0, 0):
            c.start()

    @pl.when(step + 1 < n_steps)
    def _():
        for c in copies(step + 1, 1 - slot):
            c.start()

    qb = q_ref[0].astype(BF16)
    rowc = jnp.sum(mr_ref[...] * lfr_ref[0], axis=-1, keepdims=True)

    @pl.when(grp == 0)
    def _():
        m_scr[...] = jnp.full(m_scr.shape, NEG_BIG, F32)
        l_scr[...] = jnp.zeros(l_scr.shape, F32)
        acc_scr[...] = jnp.zeros(acc_scr.shape, F32)
        run_scr[...] = jnp.zeros(run_scr.shape, F32)
        base_scr[...] = rowc + pb_ref[...]

    for c in copies(step, slot):
        c.wait()

    n_rows = g_pages * SUBLANES
    lf = lfbuf[slot].reshape(n_rows, LANES)
    within = jnp.dot(lf, cl_ref[...], precision=hi, preferred_element_type=F32)
    rowtot = jnp.dot(lf, cs_ref[...], precision=hi, preferred_element_type=F32)
    later_rows = jnp.dot(rl_ref[...], rowtot, precision=hi, preferred_element_type=F32)
    g_past = (within + later_rows) + run_scr[...]
    run_scr[...] = run_scr[...] + jnp.sum(rowtot, axis=0, keepdims=True)

    pg_rows = PAGE_SIZE * D_HEADS
    n_part = max(g_pages // SAMPLE_SPLIT, 1)
    parts = [(p0, min(p0 + n_part, g_pages)) for p0 in range(0, g_pages, n_part)]
    scores = [_dot_nt(qb, kbuf[slot, p0:p1].reshape((p1 - p0) * pg_rows, LANES).astype(BF16)) * scale
              for p0, p1 in parts]
    base = base_scr[...]
    m_run, l_run = m_scr[...], l_scr[...]
    for (p0, p1), s_part in zip(parts, scores):
        blocks = []
        s_max = None
        for j in range((p1 - p0) * SUBLANES):
            row = p0 * SUBLANES + j
            sj = (s_part[:, j * LANES:(j + 1) * LANES] + base) + g_past[row:row + 1, :]
            blocks.append(sj)
            s_max = sj if s_max is None else jnp.maximum(s_max, sj)
        m_new = jnp.maximum(m_run, jnp.max(s_max, axis=-1, keepdims=True))
        alpha = jnp.exp(m_run - m_new)
        p_sum = None
        p_blocks = []
        for sj in blocks:
            pj = jnp.exp(sj - m_new)
            p_sum = pj if p_sum is None else p_sum + pj
            p_blocks.append(pj.astype(BF16))
        l_run = alpha * l_run + jnp.sum(p_sum, axis=-1, keepdims=True)
        acc_scr[...] = alpha * acc_scr[...] + _dot(
            jnp.concatenate(p_blocks, axis=1),
            vbuf[slot, p0:p1].reshape((p1 - p0) * pg_rows, LANES).astype(BF16))
        m_run = m_new
    m_scr[...] = m_run
    l_scr[...] = l_run

    @pl.when(grp == n_groups - 1)
    def _():
        c_new = jnp.sum(mc_ref[...] * lfc_ref[0], axis=0, keepdims=True)
        s = (_dot_nt(qb, kn_ref[0].astype(BF16)) * scale + rowc) - c_new + nb_ref[...]
        m_old = m_scr[...]
        m_new = jnp.maximum(m_old, jnp.max(s, axis=-1, keepdims=True))
        alpha = jnp.exp(m_old - m_new)
        pr = jnp.exp(s - m_new)
        l_fin = alpha * l_scr[...] + jnp.sum(pr, axis=-1, keepdims=True)
        acc = alpha * acc_scr[...] + _dot(pr.astype(BF16), vn_ref[0].astype(BF16))
        o_ref[0] = (acc / l_fin) * _silu(dg_ref[0])


def _attn_sample(q_hq, kn, vn, lf_row, lf_col, dg_hq, cache_k, cache_v, cache_lf, page_table, *, layer_slot):
    bd, r, dh = q_hq.shape
    n_pages = page_table.shape[1]
    n_new = r // D_HEADS
    n_slots, n_phys = cache_k.shape[:2]
    pg_rows = PAGE_SIZE * D_HEADS
    ck = cache_k.reshape(n_slots, n_phys, pg_rows, dh)
    cv = cache_v.reshape(n_slots, n_phys, pg_rows, dh)
    clf = cache_lf.reshape(n_slots, n_phys, SUBLANES, LANES)
    g_pages = min(PAGES_PER_STEP, n_pages)
    assert n_pages % g_pages == 0
    n_groups = n_pages // g_pages
    tabs = [jnp.asarray(t) for t in _attn_sample_tables(n_new, g_pages)]

    def per_b(shape):
        return pl.BlockSpec((1,) + shape, lambda s, pt: (s // n_groups, 0, 0))

    def const(t):
        return pl.BlockSpec(t.shape, lambda s, pt: (0, 0))

    hbm = pl.BlockSpec(memory_space=pl.ANY)
    grid_spec = pltpu.PrefetchScalarGridSpec(
        num_scalar_prefetch=1,
        grid=(bd * n_groups,),
        in_specs=[per_b((r, dh)), per_b((r, dh)), per_b((r, dh)), per_b((1, r)), per_b((r, 1)), per_b((r, dh))]
        + [const(t) for t in tabs] + [hbm, hbm, hbm],
        out_specs=per_b((r, dh)),
        scratch_shapes=[
            pltpu.VMEM((2, g_pages, pg_rows, dh), F32), pltpu.VMEM((2, g_pages, pg_rows, dh), F32),
            pltpu.VMEM((2, g_pages, SUBLANES, LANES), F32), pltpu.SemaphoreType.DMA((3, 2)),
            pltpu.VMEM((r, 1), F32), pltpu.VMEM((r, 1), F32), pltpu.VMEM((r, dh), F32),
            pltpu.VMEM((1, LANES), F32), pltpu.VMEM((r, LANES), F32)],
    )
    return pl.pallas_call(
        functools.partial(_attn_sample_kernel, layer_slot=layer_slot, n_groups=n_groups, g_pages=g_pages),
        grid_spec=grid_spec,
        out_shape=jax.ShapeDtypeStruct((bd, r, dh), F32),
        compiler_params=_cparams("arbitrary"),
        name="attn_sample",
    )(page_table, q_hq, kn, vn, lf_row, lf_col, dg_hq, *tabs, ck, cv, clf)


TM_NORM = 1024
TM_IN = 2048
TN_IN_EVEN = 512
TN_IN_ODD = 256
TM_OUT = 512
TQ_POOL = 512
TQ_ATTN = 256


def kernel(x_prompt, x_sample, c_prompt, c_sample, state_ret, state_pool, cache_k, cache_v, cache_logf,
           page_table, g_pre, g_post, w_ada, b_ada, w_in_even, w_out_even, ln_a_g, ln_a_b, w_s, b_s,
           ret_g, w_in_odd, b_f, w_out_odd, w_pool, pool_scale):
    bp, seq, d = x_prompt.shape
    bd, n_new, _ = x_sample.shape
    depth = g_pre.shape[0]
    n_odd = w_in_odd.shape[0]
    w = d // 2
    n_past = page_table.shape[1] * PAGE_SIZE
    rows_s = bd * n_new

    c_all = jnp.concatenate([c_prompt, c_sample], axis=0)
    c_all = jnp.pad(c_all, ((0, -c_all.shape[0] % (2 * SUBLANES)), (0, 0)))
    mod = _ada_mod(c_all, w_ada, b_ada)

    def mods(l):
        mp = mod[l, :bp].reshape(bp, 1, 3 * d)
        ms = jnp.repeat(mod[l, bp:bp + bd], n_new, axis=0).reshape(1, rows_s, 3 * d)
        return [(m[..., :d], m[..., d:2 * d], m[..., 2 * d:]) for m in (mp, ms)]

    xp = x_prompt.reshape(bp * seq, d)
    xs = x_sample.reshape(rows_s, d)
    ret_p, ret_s, gv_s, pool_p, pool_s, lfp_l, lfs_l = [], [], [], [], [], [], []
    kv_p = kv_s = None
    zeros_state = jnp.zeros((1, bp) + state_ret.shape[2:], F32)
    w_out_even_bf, w_out_odd_bf, w_pool_bf = w_out_even.astype(BF16), w_out_odd.astype(BF16), w_pool.astype(BF16)
    w_in_odd_t = jnp.swapaxes(w_in_odd, 1, 2)
    all_mods = [mods(l) for l in range(depth)]
    (shift_p, scale_p, _), (shift_s, scale_s, _) = all_mods[0]
    g0 = g_pre[0].reshape(1, d)
    hp = _modnorm_call(xp, scale_p, shift_p, g0, per_row=False, rows_per_mod=seq, tm=TM_NORM)
    hs = _modnorm_call(xs, scale_s, shift_s, g0, per_row=True, rows_per_mod=1, tm=rows_s)
    for l in range(depth):
        j = l // 2
        (_, _, gate_p), (_, _, gate_s) = all_mods[l]
        gpost = g_post[l].reshape(1, d)
        if l % 2 == 0:
            w_out = w_out_even_bf
            zp, zs = _inproj_even(hp, hs, w_in_even, layer_slot=j, tm=TM_IN, tn=TN_IN_EVEN)
            mix = functools.partial(_even_mixer, ln_g=ln_a_g[j], ln_b=ln_a_b[j], w_s=w_s[j], b_s=b_s[j],
                                    ret_g=ret_g[j])
            ap, rp, sp = mix(zp, zeros_state, s0_slot=0, n_batch=bp, seq=seq, pos0=0, emit_vn=False,
                             out_dtype=BF16)
            as_, rs, ss, vn_s = mix(zs, state_ret, s0_slot=j, n_batch=bd, seq=n_new, pos0=n_past, emit_vn=True,
                                    out_dtype=F32)
            ret_p.append(sp)
            ret_s.append(ss)
            gv_s.append(vn_s.reshape(bd, n_new, w))
        else:
            wf = jnp.pad(w_in_odd_t[j, 6 * w:, :], ((0, LANES - D_HEADS), (0, 0)))
            bf = jnp.pad(b_f[j], (0, LANES - D_HEADS)).reshape(1, LANES)
            w_out = w_out_odd_bf
            wp = w_pool_bf
            kv_prev = None if kv_p is None else kv_p + kv_s
            zp, kp_all, vp_all, lfp, zs, ks_all, vs_all, lfs = _inproj_odd(
                hp, hs, w_in_odd_t, wf, bf, kv_prev, layer_slot=j, n_slots=n_odd, tm=TM_IN, tn=TN_IN_ODD)
            kv_p = (kp_all, vp_all)
            kv_s = (ks_all, vs_all)
            ap = _pool_branch(zp, jnp.zeros((bp, POOL_PREV, w), F32), _pool_counts(0, seq), wp, pool_scale[j],
                              layer_slot=j, n_batch=bp, seq=seq, tq=TQ_POOL, out_dtype=BF16)
            fcol, frow = _fcum(lfp, n_batch=bp, seq=seq, blk=TQ_ATTN)
            rp = _attn_prompt(zp, kp_all, vp_all, fcol, frow, layer_slot=j, n_batch=bp, seq=seq, tq=TQ_ATTN,
                              out_dtype=BF16)
            pool_p.append(zp.reshape(bp, seq, 4 * w)[:, seq - POOL_BUF:, :w])
            lfp_l.append(lfp[:, :D_HEADS].reshape(bp, seq, D_HEADS))
            prev = jnp.pad(state_pool[j], ((0, 0), (POOL_PREV - POOL_BUF, 0), (0, 0)))
            as_ = _pool_branch(zs, prev, _pool_counts(n_past, POOL_PREV), wp, pool_scale[j],
                               layer_slot=j, n_batch=bd, seq=n_new, tq=n_new, out_dtype=F32)
            zs4 = zs.reshape(bd, n_new, 4, D_HEADS, LANES)
            to_hq = lambda a: a.transpose(0, 2, 1, 3).reshape(bd, D_HEADS * n_new, LANES)
            lf_new = lfs[:, :D_HEADS].reshape(bd, n_new * D_HEADS)
            o_hq = _attn_sample(to_hq(zs4[:, :, 2]), ks_all[j].reshape(bd, n_new * D_HEADS, LANES),
                                vs_all[j].reshape(bd, n_new * D_HEADS, LANES),
                                lf_new.reshape(bd, 1, -1), lf_new.reshape(bd, -1, 1), to_hq(zs4[:, :, 3]),
                                cache_k, cache_v, cache_logf, page_table, layer_slot=j)
            rs = o_hq.reshape(bd, D_HEADS, n_new, LANES).transpose(0, 2, 1, 3).reshape(rows_s, w)
            c_in_s = zs[:, :w].reshape(bd, n_new, w)
            pool_s.append(jnp.concatenate([state_pool[j], c_in_s], axis=1)[:, -POOL_BUF:])
            lfs_l.append(lfs[:, :D_HEADS].reshape(bd, n_new, D_HEADS))
        next_mod = None
        if l + 1 < depth:
            (shift_p, scale_p, _), (shift_s, scale_s, _) = all_mods[l + 1]
            next_mod = ((scale_p, shift_p), (scale_s, shift_s), g_pre[l + 1].reshape(1, d))
        xp, hp, xs, hs = _outproj((ap, rp, xp, gate_p), (as_, rs, xs, gate_s), gpost, w_out, next_mod,
                                  layer_slot=j, rows_per_mod=seq, tm=TM_OUT)

    dh = LANES
    return (xp.reshape(bp, seq, d), xs.reshape(bd, n_new, d),
            jnp.stack(ret_p), jnp.stack(ret_s), jnp.stack(gv_s), jnp.stack(pool_p), jnp.stack(pool_s),
            kv_p[0].reshape(n_odd, bp, seq, D_HEADS, dh), kv_p[1].reshape(n_odd, bp, seq, D_HEADS, dh),
            jnp.stack(lfp_l),
            kv_s[0].reshape(n_odd, bd, n_new, D_HEADS, dh), kv_s[1].reshape(n_odd, bd, n_new, D_HEADS, dh),
            jnp.stack(lfs_l))
```

```python
import functools

import numpy as np
import jax
import jax.numpy as jnp
from jax import lax
from jax.experimental import pallas as pl
from jax.experimental.pallas import tpu as pltpu

F32 = jnp.float32
BF16 = jnp.bfloat16

EPS = 1e-6
ROPE_BASE = 10000.0
CHUNK = 128
A_GROUPS = 8
B_HEADS = 4
POOL_WINDOWS = (2, 4, 8, 16)
POOL_BUF = 15
D_HEADS = 8
PAGE_SIZE = 128
LANES = 128
SUBLANES = 8
NEG_BIG = -1e30
VMEM_LIMIT = 56 * 1024 * 1024


def _cparams(*sem):
    return pltpu.CompilerParams(dimension_semantics=sem, vmem_limit_bytes=VMEM_LIMIT)


def _silu(x):
    return x * jax.nn.sigmoid(x)


def _log_sigmoid(x):
    return jnp.minimum(x, 0.0) - jnp.log1p(jnp.exp(-jnp.abs(x)))


def _dot(a, b):
    return jnp.dot(a, b, preferred_element_type=F32)


def _dot_nt(a, b):
    return lax.dot_general(a, b, (((1,), (1,)), ((), ())), preferred_element_type=F32)


def _dot_tn(a, b):
    return lax.dot_general(a, b, (((0,), (0,)), ((), ())), preferred_element_type=F32)


def _ada_kernel(c_ref, w_ref, b_ref, o_ref):
    a = _silu(c_ref[...]).astype(BF16)
    o_ref[0] = _dot(a, w_ref[0].astype(BF16)) + b_ref[0]


def _ada_mod(c_all, w_ada, b_ada, tn=1024):
    depth, d, n = w_ada.shape
    r = c_all.shape[0]
    return pl.pallas_call(
        _ada_kernel,
        grid=(depth, n // tn),
        in_specs=[
            pl.BlockSpec((r, d), lambda l, j: (0, 0)),
            pl.BlockSpec((1, d, tn), lambda l, j: (l, 0, j)),
            pl.BlockSpec((1, 1, tn), lambda l, j: (l, 0, j)),
        ],
        out_specs=pl.BlockSpec((1, r, tn), lambda l, j: (l, 0, j)),
        out_shape=jax.ShapeDtypeStruct((depth, r, n), F32),
        compiler_params=_cparams("arbitrary", "arbitrary"),
        name="ada_mod",
    )(c_all, w_ada, b_ada.reshape(depth, 1, n))


def _modnorm(x, scale, shift, g):
    y = x * lax.rsqrt(jnp.mean(x * x, axis=-1, keepdims=True) + EPS)
    return (y * (g * (1.0 + scale)) + shift).astype(BF16)


def _modnorm_kernel(x_ref, sc_ref, sh_ref, g_ref, h_ref):
    h_ref[...] = _modnorm(x_ref[...], sc_ref[0], sh_ref[0], g_ref[...])


def _mod_spec(per_row, rows_per_mod, tm, d):
    if per_row:
        return pl.BlockSpec((1, tm, d), lambda i: (0, i, 0))
    return pl.BlockSpec((1, 1, d), lambda i: ((i * tm) // rows_per_mod, 0, 0))


def _modnorm_call(x2d, scale, shift, g, *, per_row, rows_per_mod, tm):
    rows, d = x2d.shape
    return pl.pallas_call(
        _modnorm_kernel,
        grid=(rows // tm,),
        in_specs=[pl.BlockSpec((tm, d), lambda i: (i, 0)),
                  _mod_spec(per_row, rows_per_mod, tm, d), _mod_spec(per_row, rows_per_mod, tm, d),
                  pl.BlockSpec((1, d), lambda i: (0, 0))],
        out_specs=pl.BlockSpec((tm, d), lambda i: (i, 0)),
        out_shape=jax.ShapeDtypeStruct((rows, d), BF16),
        compiler_params=_cparams("arbitrary"),
        name="modnorm",
    )(x2d, scale, shift, g)


def _inproj_even_kernel(h_ref, hs_ref, w_ref, z_ref, zs_ref):
    wb = w_ref[0].astype(BF16)
    z_ref[...] = _dot(h_ref[...], wb)

    @pl.when(pl.program_id(0) == 0)
    def _():
        zs_ref[...] = _dot(hs_ref[...], wb)


def _inproj_odd_kernel(h_ref, hs_ref, w_ref, wf_ref, bf_ref, *rest, tiles_per_group, kv_slot):
    z_ref, k_ref, v_ref, lf_ref, zs_ref, ks_ref, vs_ref, lfs_ref = rest[-8:]
    j = pl.program_id(1)
    group = j // tiles_per_group

    def emit(x_ref, z_o, k_o, v_o, lf_o):
        def project():
            return _dot_nt(x_ref[...], w_ref[0].astype(BF16))

        def put(kv_ref):
            for s in range(kv_ref.shape[0]):
                kv_ref[s] = project() if s == kv_slot else jnp.zeros(kv_ref.shape[1:], F32)

        @pl.when(j == 0)
        def _():
            lf_o[...] = _log_sigmoid(_dot_nt(x_ref[...], wf_ref[...].astype(BF16)) + bf_ref[...])

        @pl.when(jnp.logical_or(group < 3, group == 5))
        def _():
            z_o[...] = project()

        @pl.when(group == 3)
        def _():
            put(k_o)

        @pl.when(group == 4)
        def _():
            put(v_o)

    emit(h_ref, z_ref, k_ref, v_ref, lf_ref)

    @pl.when(pl.program_id(0) == 0)
    def _():
        emit(hs_ref, zs_ref, ks_ref, vs_ref, lfs_ref)


def _inproj_even(h, hs, w_all, *, layer_slot, tm, tn):
    rows, d = h.shape
    rows_s = hs.shape[0]
    n = w_all.shape[2]
    n_j = n // tn
    return pl.pallas_call(
        _inproj_even_kernel,
        grid=(rows // tm, n_j),
        in_specs=[
            pl.BlockSpec((tm, d), lambda i, j: (i, 0)),
            pl.BlockSpec((rows_s, d), lambda i, j: (0, 0)),
            pl.BlockSpec((1, d, tn), lambda i, j: (layer_slot, 0, j)),
        ],
        out_specs=[pl.BlockSpec((tm, tn), lambda i, j: (i, j)),
                   pl.BlockSpec((rows_s, tn), lambda i, j: (0, jnp.where(i == 0, j, n_j - 1)))],
        out_shape=[jax.ShapeDtypeStruct((rows, n), F32), jax.ShapeDtypeStruct((rows_s, n), F32)],
        compiler_params=_cparams("arbitrary", "arbitrary"),
        name="inproj_even",
    )(h, hs, w_all)


def _inproj_odd(h, hs, w_all_t, wf, bf, kv_prev, *, layer_slot, n_slots, tm, tn):
    rows, d = h.shape
    rows_s = hs.shape[0]
    w = (w_all_t.shape[1] - D_HEADS) // 6
    tpg = w // tn
    n_tiles = 6 * tpg

    creates_kv = kv_prev is None
    kv_slots = n_slots if creates_kv else 1
    kv_first = 0 if creates_kv else layer_slot

    def zcol(j):
        return j - jnp.clip(j - (3 * tpg - 1), 0, 2 * tpg)

    def kcol(j):
        return jnp.clip(j - 3 * tpg, 0, tpg - 1)

    def vcol(j):
        return jnp.clip(j - 4 * tpg, 0, tpg - 1)

    def sample_j(i, j):
        return jnp.where(i == 0, j, n_tiles - 1)

    in_specs = [
        pl.BlockSpec((tm, d), lambda i, j: (i, 0)),
        pl.BlockSpec((rows_s, d), lambda i, j: (0, 0)),
        pl.BlockSpec((1, tn, d), lambda i, j: (layer_slot, j, 0)),
        pl.BlockSpec((LANES, d), lambda i, j: (0, 0)),
        pl.BlockSpec((1, LANES), lambda i, j: (0, 0)),
    ]
    args = [h, hs, w_all_t, wf, bf]
    aliases = {}
    if kv_prev is not None:
        in_specs += [pl.BlockSpec(memory_space=pl.ANY)] * 4
        aliases = {len(args): 1, len(args) + 1: 2, len(args) + 2: 5, len(args) + 3: 6}
        args += list(kv_prev)
    return pl.pallas_call(
        functools.partial(_inproj_odd_kernel, tiles_per_group=tpg, kv_slot=layer_slot if creates_kv else 0),
        grid=(rows // tm, n_tiles),
        in_specs=in_specs,
        out_specs=[
            pl.BlockSpec((tm, tn), lambda i, j: (i, zcol(j))),
            pl.BlockSpec((kv_slots, tm, tn), lambda i, j: (kv_first, i, kcol(j))),
            pl.BlockSpec((kv_slots, tm, tn), lambda i, j: (kv_first, i, vcol(j))),
            pl.BlockSpec((tm, LANES), lambda i, j: (i, 0)),
            pl.BlockSpec((rows_s, tn), lambda i, j: (0, zcol(sample_j(i, j)))),
            pl.BlockSpec((kv_slots, rows_s, tn), lambda i, j: (kv_first, 0, kcol(sample_j(i, j)))),
            pl.BlockSpec((kv_slots, rows_s, tn), lambda i, j: (kv_first, 0, vcol(sample_j(i, j)))),
            pl.BlockSpec((rows_s, LANES), lambda i, j: (0, 0)),
        ],
        out_shape=[
            jax.ShapeDtypeStruct((rows, 4 * w), F32),
            jax.ShapeDtypeStruct((n_slots, rows, w), F32),
            jax.ShapeDtypeStruct((n_slots, rows, w), F32),
            jax.ShapeDtypeStruct((rows, LANES), F32),
            jax.ShapeDtypeStruct((rows_s, 4 * w), F32),
            jax.ShapeDtypeStruct((n_slots, rows_s, w), F32),
            jax.ShapeDtypeStruct((n_slots, rows_s, w), F32),
            jax.ShapeDtypeStruct((rows_s, LANES), F32),
        ],
        input_output_aliases=aliases,
        compiler_params=_cparams("arbitrary", "arbitrary"),
        name="inproj_odd",
    )(*args)


def _outproj_kernel(a_ref, b_ref, x_ref, gate_ref, as_ref, bs_ref, xs_ref, gates_ref, g_ref, w1_ref, w2_ref,
                    *rest, with_next):
    if with_next:
        sc_ref, sh_ref, scs_ref, shs_ref, gn_ref, o_ref, h_ref, os_ref, hs_ref = rest
    else:
        o_ref, os_ref = rest

    def finish(a_r, b_r, x_r, gate_r, o_r, next_refs):
        y = _dot(a_r[...].astype(BF16), w1_ref[0]) + _dot(b_r[...].astype(BF16), w2_ref[0])
        yn = y * lax.rsqrt(jnp.mean(y * y, axis=-1, keepdims=True) + EPS)
        x_new = x_r[...] + yn * (gate_r[0] * g_ref[...])
        if next_refs is not None:
            sc_r, sh_r, h_r = next_refs
            h_r[...] = _modnorm(x_new, sc_r[0], sh_r[0], gn_ref[...])
        o_r[...] = x_new

    finish(a_ref, b_ref, x_ref, gate_ref, o_ref, (sc_ref, sh_ref, h_ref) if with_next else None)

    @pl.when(pl.program_id(0) == 0)
    def _():
        finish(as_ref, bs_ref, xs_ref, gates_ref, os_ref, (scs_ref, shs_ref, hs_ref) if with_next else None)


def _outproj(prompt, sample, g, w_bf, next_mod, *, layer_slot, rows_per_mod, tm):
    a, b, x2d, gate = prompt
    a_s, b_s, xs2d, gate_s = sample
    rows, d = x2d.shape
    rows_s = xs2d.shape[0]
    w = a.shape[1]
    mod_spec = _mod_spec(False, rows_per_mod, tm, d)
    mod_s_spec = pl.BlockSpec((1, rows_s, d), lambda i: (0, 0, 0))
    row_spec = pl.BlockSpec((tm, d), lambda i: (i, 0))
    row_s_spec = pl.BlockSpec((rows_s, d), lambda i: (0, 0))
    half_s_spec = pl.BlockSpec((rows_s, w), lambda i: (0, 0))
    vec_spec = pl.BlockSpec((1, d), lambda i: (0, 0))
    in_specs = [
        pl.BlockSpec((tm, w), lambda i: (i, 0)),
        pl.BlockSpec((tm, w), lambda i: (i, 0)),
        row_spec, mod_spec,
        half_s_spec, half_s_spec, row_s_spec, mod_s_spec,
        vec_spec,
        pl.BlockSpec((1, w, d), lambda i: (layer_slot, 0, 0)),
        pl.BlockSpec((1, w, d), lambda i: (layer_slot, 1, 0)),
    ]
    args = [a, b, x2d, gate, a_s, b_s, xs2d, gate_s, g, w_bf, w_bf]
    out_specs = [row_spec]
    out_shape = [jax.ShapeDtypeStruct((rows, d), F32)]
    if next_mod is not None:
        (sc_p, sh_p), (sc_s, sh_s), g_next = next_mod
        in_specs += [mod_spec, mod_spec, mod_s_spec, mod_s_spec, vec_spec]
        args += [sc_p, sh_p, sc_s, sh_s, g_next]
        out_specs.append(row_spec)
        out_shape.append(jax.ShapeDtypeStruct((rows, d), BF16))
    out_specs.append(row_s_spec)
    out_shape.append(jax.ShapeDtypeStruct((rows_s, d), F32))
    if next_mod is not None:
        out_specs.append(row_s_spec)
        out_shape.append(jax.ShapeDtypeStruct((rows_s, d), BF16))
    res = pl.pallas_call(
        functools.partial(_outproj_kernel, with_next=next_mod is not None),
        grid=(rows // tm,),
        in_specs=in_specs,
        out_specs=out_specs,
        out_shape=out_shape,
        compiler_params=_cparams("arbitrary"),
        name="outproj",
    )(*args)
    if next_mod is not None:
        xp, hp, xs, hs = res
        return xp, hp, xs, hs
    xp, xs = res
    return xp, None, xs, None


def _retention_tables(c_len, pos0, n_rows):
    lg = np.log(1.0 - 2.0 ** (-5.0 - np.arange(B_HEADS, dtype=np.float64)))
    t = np.arange(CHUNK, dtype=np.float64)
    diff = t[:, None] - t[None, :]
    dmask = np.where(diff >= 0, np.exp(lg[:, None, None] * np.maximum(diff, 0.0)), 0.0).astype(np.float32)
    qdec = np.exp(lg[None, :] * (t + 1.0)[:, None]).astype(np.float32)
    kdec = np.exp(lg[None, :] * np.maximum(c_len - 1.0 - t, 0.0)[:, None]).astype(np.float32)
    cdec = tuple(float(v) for v in np.exp(lg * c_len).astype(np.float32))
    half = LANES
    inv = ROPE_BASE ** (-np.arange(half, dtype=np.float64) / half)
    pos = (pos0 + np.arange(n_rows)).astype(np.float64)
    ang = pos[:, None] * inv[None, :]
    return dmask, qdec, kdec, cdec, ang


MIXER_CHUNKS_PER_STEP = 4


def _even_mixer_kernel(u_ref, v_ref, ga_ref, q_ref, k_ref, vv_ref, gr_ref,
                       lng_ref, lnb_ref, ws_ref, bst_ref, retg_ref, cos_ref, sin_ref,
                       dmask_ref, qdec_ref, kdec_ref, s0_ref, *rest, c_in, cdec, emit_vn):
    if emit_vn:
        ao_ref, ro_ref, so_ref, vn_ref, s_scr = rest
    else:
        ao_ref, ro_ref, so_ref, s_scr = rest
    c = pl.program_id(1)

    @pl.when(c == 0)
    def _():
        s_scr[...] = s0_ref[0, 0]

    n_sub = max(u_ref.shape[0] // CHUNK, 1)
    w = u_ref.shape[1]
    gw = w // A_GROUPS
    dk = w // B_HEADS
    half = dk // 2
    causal = (lax.broadcasted_iota(jnp.int32, (CHUNK, CHUNK), 0)
              >= lax.broadcasted_iota(jnp.int32, (CHUNK, CHUNK), 1))
    wms = [jnp.where(causal, ws_ref[g], 0.0).astype(BF16) for g in range(A_GROUPS)]
    for cc in range(n_sub):
        r0 = cc * CHUNK
        n_out = min(c_in, CHUNK)
        rows = slice(r0, r0 + n_out)

        def ld(ref):
            if c_in >= CHUNK:
                return ref[r0:r0 + CHUNK, :]
            return jnp.concatenate([ref[...], jnp.zeros((CHUNK - c_in, w), F32)], axis=0)

        v = ld(v_ref)
        xc = v - jnp.mean(v, axis=-1, keepdims=True)
        vn = xc * lax.rsqrt(jnp.mean(xc * xc, axis=-1, keepdims=True) + EPS) * lng_ref[...] + lnb_ref[...]
        if emit_vn:
            vn_ref[rows, :] = vn[:n_out]
        vnb = vn.astype(BF16)
        u = ld(u_ref)
        ga = ld(ga_ref)
        for g in range(A_GROUPS):
            sl = slice(g * gw, (g + 1) * gw)
            mixed = _dot(wms[g], vnb[:, sl]) + bst_ref[:, g:g + 1]
            ao_ref[rows, sl] = (u[:, sl] * mixed * _silu(ga[:, sl]))[:n_out].astype(ao_ref.dtype)

        cos = cos_ref[r0:r0 + CHUNK, :]
        sin = sin_ref[r0:r0 + CHUNK, :]
        q = ld(q_ref)
        k = ld(k_ref)
        vv = ld(vv_ref)
        gr = ld(gr_ref)
        for h in range(B_HEADS):
            sl = slice(h * dk, (h + 1) * dk)
            q1, q2 = q[:, h * dk:h * dk + half], q[:, h * dk + half:(h + 1) * dk]
            k1, k2 = k[:, h * dk:h * dk + half], k[:, h * dk + half:(h + 1) * dk]
            qr = jnp.concatenate([q1 * cos - q2 * sin, q1 * sin + q2 * cos], axis=1)
            kr = jnp.concatenate([k1 * cos - k2 * sin, k1 * sin + k2 * cos], axis=1) * (dk ** -0.5)
            qrb = qr.astype(BF16)
            vb = vv[:, sl].astype(BF16)
            inner = _dot_nt(qrb, kr.astype(BF16)) * dmask_ref[h]
            s = s_scr[h]
            o = _dot(inner.astype(BF16), vb) + _dot(qrb, s.astype(BF16)) * qdec_ref[:, h:h + 1]
            kd = (kr * kdec_ref[:, h:h + 1]).astype(BF16)
            s_scr[h] = s * cdec[h] + _dot_tn(kd, vb)
            on = o * lax.rsqrt(jnp.mean(o * o, axis=-1, keepdims=True) + EPS) * retg_ref[:, sl]
            ro_ref[rows, sl] = (on * _silu(gr[:, sl]))[:n_out].astype(ro_ref.dtype)

    @pl.when(c == pl.num_programs(1) - 1)
    def _():
        so_ref[0] = s_scr[...]


def _even_mixer(z, s0_all, ln_g, ln_b, w_s, b_s, ret_g, *, s0_slot, n_batch, seq, pos0, emit_vn, out_dtype):
    rows, n = z.shape
    s_shape = s0_all.shape[2:]
    w = n // 7
    c_len = CHUNK if seq % CHUNK == 0 else seq
    n_chunks_total = seq // min(seq, CHUNK)
    per_step = min(MIXER_CHUNKS_PER_STEP, n_chunks_total)
    assert n_chunks_total % per_step == 0
    c_in = min(seq, CHUNK) * per_step
    n_chunks = n_chunks_total // per_step
    dmask, qdec, kdec, cdec, ang = _retention_tables(c_len, pos0, n_chunks_total * CHUNK)
    cos = jnp.asarray(np.cos(ang).astype(np.float32))
    sin = jnp.asarray(np.sin(ang).astype(np.float32))

    def zspec(col):
        return pl.BlockSpec((c_in, w), lambda b, c, col=col: (b * n_chunks + c, col))

    def const(shape):
        return pl.BlockSpec(shape, lambda b, c: (0,) * len(shape))

    row_spec = pl.BlockSpec((c_in, w), lambda b, c: (b * n_chunks + c, 0))
    state_in_spec = pl.BlockSpec((1, 1) + s_shape, lambda b, c: (s0_slot, b, 0, 0, 0))
    state_spec = pl.BlockSpec((1,) + s_shape, lambda b, c: (b, 0, 0, 0))
    out_specs = [row_spec, row_spec, state_spec]
    out_shape = [jax.ShapeDtypeStruct((rows, w), out_dtype),
                 jax.ShapeDtypeStruct((rows, w), out_dtype),
                 jax.ShapeDtypeStruct((n_batch,) + s_shape, F32)]
    if emit_vn:
        out_specs.append(row_spec)
        out_shape.append(jax.ShapeDtypeStruct((rows, w), F32))
    return pl.pallas_call(
        functools.partial(_even_mixer_kernel, c_in=c_in, cdec=cdec, emit_vn=emit_vn),
        grid=(n_batch, n_chunks),
        in_specs=[zspec(i) for i in range(7)] + [
            const((1, w)), const((1, w)), const(w_s.shape), const((CHUNK, A_GROUPS)), const((1, w)),
            pl.BlockSpec((per_step * CHUNK, LANES), lambda b, c: (c, 0)),
            pl.BlockSpec((per_step * CHUNK, LANES), lambda b, c: (c, 0)),
            const(dmask.shape), const(qdec.shape), const(kdec.shape),
            state_in_spec,
        ],
        out_specs=out_specs,
        out_shape=out_shape,
        scratch_shapes=[pltpu.VMEM(s_shape, F32)],
        compiler_params=_cparams("arbitrary", "arbitrary"),
        name="even_mixer",
    )(z, z, z, z, z, z, z, ln_g.reshape(1, w), ln_b.reshape(1, w), w_s, b_s.T, ret_g.reshape(1, w),
      cos, sin, jnp.asarray(dmask), jnp.asarray(qdec), jnp.asarray(kdec), s0_all)


POOL_PREV = 16
POOL_LEAD = SUBLANES


def _pool_kernel(cin_ref, cg_ref, prev_ref, cnt_ref, wp_ref, ps_ref, *rest, tq, t_pad):
    o_ref, e0, e1, e2, e3, e4 = rest[-6:]
    t = pl.program_id(1)
    base = POOL_LEAD + POOL_PREV
    total = base + t_pad
    w = cin_ref.shape[1]
    gw = w // len(POOL_WINDOWS)

    @pl.when(t == 0)
    def _():
        for e in (e0, e1, e2, e3, e4):
            e[0:POOL_LEAD, :] = jnp.zeros((POOL_LEAD, w), F32)
        e0[POOL_LEAD:base, :] = prev_ref[0]
        if t_pad > tq:
            e0[base + tq:total, :] = jnp.zeros((t_pad - tq, w), F32)

    e0[base:base + tq, :] = cin_ref[...]
    levels = (e0, e1, e2, e3, e4)
    for lvl in range(1, 5):
        shift = 1 << (lvl - 1)
        c0 = (lvl - 1) * gw
        src, dst = levels[lvl - 1], levels[lvl]
        dst[POOL_LEAD:total, c0:] = (src[POOL_LEAD:total, c0:]
                                     + src[POOL_LEAD - shift:total - shift, c0:])
    cur = e0[base:total, :]
    cg = cg_ref[...]
    for gi in range(len(POOL_WINDOWS)):
        sl = slice(gi * gw, (gi + 1) * gw)
        win = levels[gi + 1][base:total, sl]
        pooled = win / cnt_ref[:, gi:gi + 1] - cur[:, sl]
        mixed = _dot(pooled.astype(BF16), wp_ref[0, gi]) * ps_ref[:, sl]
        o_ref[:, sl] = (mixed[:tq] * _silu(cg[:, sl])).astype(o_ref.dtype)
    e0[POOL_LEAD:base, :] = e0[POOL_LEAD + t_pad:base + t_pad, :]


def _pool_branch(z_main, prev, cnt, wp_bf, pool_scale, *, layer_slot, n_batch, seq, tq, out_dtype, after=None):
    rows = z_main.shape[0]
    w = wp_bf.shape[1] * wp_bf.shape[2]
    t_pad = max(tq, POOL_PREV)
    n_t = seq // tq
    ext = pltpu.VMEM((POOL_LEAD + POOL_PREV + t_pad, w), F32)
    in_specs = [
        pl.BlockSpec((tq, w), lambda b, t: (b * n_t + t, 0)),
        pl.BlockSpec((tq, w), lambda b, t: (b * n_t + t, 1)),
        pl.BlockSpec((1, POOL_PREV, w), lambda b, t: (b, 0, 0)),
        pl.BlockSpec((t_pad, len(POOL_WINDOWS)), lambda b, t: (t, 0)),
        pl.BlockSpec((1,) + wp_bf.shape[1:], lambda b, t: (layer_slot, 0, 0, 0)),
        pl.BlockSpec((1, w), lambda b, t: (0, 0)),
    ]
    args = [z_main, z_main, prev, cnt, wp_bf, pool_scale.reshape(1, w)]
    if after is not None:
        in_specs.append(pl.BlockSpec((2 * SUBLANES, LANES), lambda b, t: (0, 0)))
        args.append(after)
    return pl.pallas_call(
        functools.partial(_pool_kernel, tq=tq, t_pad=t_pad),
        grid=(n_batch, n_t),
        in_specs=in_specs,
        out_specs=pl.BlockSpec((tq, w), lambda b, t: (b * n_t + t, 0)),
        out_shape=jax.ShapeDtypeStruct((rows, w), out_dtype),
        scratch_shapes=[ext] * 5,
        compiler_params=_cparams("arbitrary", "arbitrary"),
        name="pool_branch",
    )(*args)


def _pool_counts(pos0, n_rows):
    pos = pos0 + np.arange(n_rows)
    return jnp.asarray(np.stack([np.minimum(pos + 1, wd) for wd in POOL_WINDOWS], axis=1).astype(np.float32))


def _split3(x):
    a = x.astype(BF16)
    r1 = x - a.astype(F32)
    b = r1.astype(BF16)
    c = (r1 - b.astype(F32)).astype(BF16)
    return a, b, c


def _fcum_kernel(lf_ref, fc_ref, fr_ref, *, blk):
    n_blk = lf_ref.shape[0] // blk
    tri = (lax.broadcasted_iota(jnp.int32, (blk, blk), 0)
           >= lax.broadcasted_iota(jnp.int32, (blk, blk), 1)).astype(BF16)
    carry = jnp.zeros((1, LANES), F32)
    for i in range(n_blk):
        a, b, c = _split3(lf_ref[i * blk:(i + 1) * blk, :])
        cs = (_dot(tri, a) + _dot(tri, b)) + _dot(tri, c) + carry
        carry = cs[blk - 1:blk, :]
        fc_ref[i * blk:(i + 1) * blk, :] = cs
        fr_ref[0, i] = cs.T[:D_HEADS, :]


def _fcum(lf, *, n_batch, seq, blk):
    return pl.pallas_call(
        functools.partial(_fcum_kernel, blk=blk),
        grid=(n_batch,),
        in_specs=[pl.BlockSpec((seq, LANES), lambda b: (b, 0))],
        out_specs=[pl.BlockSpec((seq, LANES), lambda b: (b, 0)),
                   pl.BlockSpec((1, seq // blk, D_HEADS, blk), lambda b: (b, 0, 0, 0))],
        out_shape=[jax.ShapeDtypeStruct((n_batch * seq, LANES), F32),
                   jax.ShapeDtypeStruct((n_batch, seq // blk, D_HEADS, blk), F32)],
        compiler_params=_cparams("arbitrary"),
        name="forget_cumsum",
    )(lf)


SCORE_LEAD = 3


def _attn_prompt_kernel(q_ref, dg_ref, k_ref, v_ref, fc_ref, fr_ref, o_ref,
                        kb_scr, vt_scr, qt_scr, *acc_scrs, tq):
    i = pl.program_id(1)
    dh = LANES
    scale = dh ** -0.5
    n_blk = k_ref.shape[1] // tq

    @pl.when(i == 0)
    def _():
        kb_scr[...] = k_ref[0].astype(BF16)
        for h in range(D_HEADS):
            for kb in range(n_blk):
                vt_scr[kb, h * dh:(h + 1) * dh, :] = (
                    v_ref[0, kb * tq:(kb + 1) * tq, h * dh:(h + 1) * dh].T.astype(BF16))

    for h in range(D_HEADS):
        qt_scr[h] = (q_ref[:, h * dh:(h + 1) * dh] * scale).T.astype(BF16)
        acc_scrs[h][...] = jnp.zeros((dh, tq), F32)
    key_le_query = (lax.broadcasted_iota(jnp.int32, (tq, tq), 0)
                    <= lax.broadcasted_iota(jnp.int32, (tq, tq), 1))

    def block(kb, stats, masked):
        r0 = pl.multiple_of(kb * tq, tq)
        new_stats = []

        def scores(h):
            sl = slice(h * dh, (h + 1) * dh)
            return _dot(kb_scr[pl.ds(r0, tq), sl], qt_scr[h]) - fc_ref[pl.ds(r0, tq), h:h + 1]

        pending = [scores(h) for h in range(SCORE_LEAD)]
        for h in range(D_HEADS):
            sl = slice(h * dh, (h + 1) * dh)
            m_old, l_old = stats[h]
            fq = fr_ref[0, 0, h:h + 1, :]
            u = pending.pop(0)
            if h + SCORE_LEAD < D_HEADS:
                pending.append(scores(h + SCORE_LEAD))
            if masked:
                u = jnp.where(key_le_query, u, NEG_BIG)
            m_new = jnp.maximum(m_old, jnp.max(u, axis=0, keepdims=True) + fq)
            alpha = jnp.exp(m_old - m_new)
            p = jnp.exp(u - (m_new - fq))
            new_stats.append((m_new, alpha * l_old + jnp.sum(p, axis=0, keepdims=True)))
            acc_scrs[h][...] = alpha * acc_scrs[h][...] + _dot(vt_scr[kb, sl, :], p.astype(BF16))
        return tuple(new_stats)

    stats0 = tuple((jnp.full((1, tq), NEG_BIG, F32), jnp.zeros((1, tq), F32)) for _ in range(D_HEADS))
    stats = lax.fori_loop(0, i, lambda kb, st: block(kb, st, False), stats0)
    stats = block(i, stats, True)
    for h in range(D_HEADS):
        sl = slice(h * dh, (h + 1) * dh)
        o_ref[:, sl] = ((acc_scrs[h][...] / stats[h][1]).T * _silu(dg_ref[:, sl])).astype(o_ref.dtype)


def _attn_prompt(z_main, k_all, v_all, fcol, frow, *, layer_slot, n_batch, seq, tq, out_dtype):
    rows = z_main.shape[0]
    w = k_all.shape[2]
    n_q = seq // tq
    return pl.pallas_call(
        functools.partial(_attn_prompt_kernel, tq=tq),
        grid=(n_batch, n_q),
        in_specs=[
            pl.BlockSpec((tq, w), lambda b, i: (b * n_q + i, 2)),
            pl.BlockSpec((tq, w), lambda b, i: (b * n_q + i, 3)),
            pl.BlockSpec((1, seq, w), lambda b, i: (layer_slot, b, 0)),
            pl.BlockSpec((1, seq, w), lambda b, i: (layer_slot, b, 0)),
            pl.BlockSpec((seq, LANES), lambda b, i: (b, 0)),
            pl.BlockSpec((1, 1, D_HEADS, tq), lambda b, i: (b, i, 0, 0)),
        ],
        out_specs=pl.BlockSpec((tq, w), lambda b, i: (b * n_q + i, 0)),
        out_shape=jax.ShapeDtypeStruct((rows, w), out_dtype),
        scratch_shapes=[pltpu.VMEM((seq, w), BF16), pltpu.VMEM((n_q, w, tq), BF16),
                        pltpu.VMEM((D_HEADS, LANES, tq), BF16)]
        + [pltpu.VMEM((LANES, tq), F32)] * D_HEADS,
        compiler_params=_cparams("arbitrary", "arbitrary"),
        name="attn_prompt",
    )(z_main, z_main, k_all, v_all, fcol, frow)


PAGES_PER_STEP = 16
SAMPLE_SPLIT = 4


def _attn_sample_tables(n_new, g_pages):
    pg = np.arange(LANES)
    same_head = (pg[:, None] % D_HEADS) == (pg[None, :] % D_HEADS)
    later = (pg[:, None] // D_HEADS) > (pg[None, :] // D_HEADS)
    c_later = (same_head & later).astype(np.float32)
    c_same = same_head.astype(np.float32)
    rows = np.arange(g_pages * SUBLANES)
    r_later = (rows[None, :] > rows[:, None]).astype(np.float32)
    r = np.arange(D_HEADS * n_new)
    past_ok = (pg[None, :] % D_HEADS) == (r[:, None] // n_new)
    past_bias = np.where(past_ok, 0.0, NEG_BIG).astype(np.float32)
    cn = np.arange(n_new * D_HEADS)
    new_ok = ((cn[None, :] % D_HEADS) == (r[:, None] // n_new)) & ((cn[None, :] // D_HEADS) <= (r[:, None] % n_new))
    new_bias = np.where(new_ok, 0.0, NEG_BIG).astype(np.float32)
    m_cols = ((cn[:, None] % D_HEADS == cn[None, :] % D_HEADS)
              & (cn[:, None] // D_HEADS <= cn[None, :] // D_HEADS)).astype(np.float32)
    m_rows = ((cn[None, :] % D_HEADS == r[:, None] // n_new)
              & (cn[None, :] // D_HEADS <= r[:, None] % n_new)).astype(np.float32)
    return c_later, c_same, r_later, past_bias, new_bias, m_cols, m_rows


def _page_copies(pt_ref, ck_hbm, cv_hbm, clf_hbm, kbuf, vbuf, lfbuf, sems, step, slot, *,
                 layer_slot, n_groups, g_pages):
    b = step // n_groups
    first_page = (n_groups - 1 - step % n_groups) * g_pages
    copies = []
    for g in range(g_pages):
        page = pt_ref[b, first_page + g]
        copies.append(pltpu.make_async_copy(ck_hbm.at[layer_slot, page], kbuf.at[slot, g], sems.at[0, slot]))
        copies.append(pltpu.make_async_copy(cv_hbm.at[layer_slot, page], vbuf.at[slot, g], sems.at[1, slot]))
        copies.append(pltpu.make_async_copy(clf_hbm.at[layer_slot, page], lfbuf.at[slot, g], sems.at[2, slot]))
    return copies


def _attn_sample_kernel(pt_ref, q_ref, kn_ref, vn_ref, lfr_ref, lfc_ref, dg_ref,
                        cl_ref, cs_ref, rl_ref, pb_ref, nb_ref, mc_ref, mr_ref,
                        ck_hbm, cv_hbm, clf_hbm, o_ref,
                        kbuf, vbuf, lfbuf, sems, m_scr, l_scr, acc_scr, run_scr, base_scr, *,
                        layer_slot, n_groups, g_pages):
    step = pl.program_id(0)
    n_steps = pl.num_programs(0)
    slot = step % 2
    grp = step % n_groups
    scale = LANES ** -0.5
    hi = lax.Precision.HIGHEST
    copies = functools.partial(_page_copies, pt_ref, ck_hbm, cv_hbm, clf_hbm, kbuf, vbuf, lfbuf, sems,
                               layer_slot=layer_slot, n_groups=n_groups, g_pages=g_pages)

    @pl.when(step == 0)
    def _():
        for c in copies(0, 0):
            c.start()

    @pl.when(step + 1 < n_steps)
    def _():
        for c in copies(step + 1, 1 - slot):
            c.start()

    qb = q_ref[0].astype(BF16)
    rowc = jnp.sum(mr_ref[...] * lfr_ref[0], axis=-1, keepdims=True)

    @pl.when(grp == 0)
    def _():
        m_scr[...] = jnp.full(m_scr.shape, NEG_BIG, F32)
        l_scr[...] = jnp.zeros(l_scr.shape, F32)
        acc_scr[...] = jnp.zeros(acc_scr.shape, F32)
        run_scr[...] = jnp.zeros(run_scr.shape, F32)
        base_scr[...] = rowc + pb_ref[...]

    for c in copies(step, slot):
        c.wait()

    n_rows = g_pages * SUBLANES
    lf = lfbuf[slot].reshape(n_rows, LANES)
    within = jnp.dot(lf, cl_ref[...], precision=hi, preferred_element_type=F32)
    rowtot = jnp.dot(lf, cs_ref[...], precision=hi, preferred_element_type=F32)
    later_rows = jnp.dot(rl_ref[...], rowtot, precision=hi, preferred_element_type=F32)
    g_past = (within + later_rows) + run_scr[...]
    run_scr[...] = run_scr[...] + jnp.sum(rowtot, axis=0, keepdims=True)

    pg_rows = PAGE_SIZE * D_HEADS
    n_part = max(g_pages // SAMPLE_SPLIT, 1)
    parts = [(p0, min(p0 + n_part, g_pages)) for p0 in range(0, g_pages, n_part)]
    scores = [_dot_nt(qb, kbuf[slot, p0:p1].reshape((p1 - p0) * pg_rows, LANES).astype(BF16)) * scale
              for p0, p1 in parts]
    base = base_scr[...]
    m_run, l_run = m_scr[...], l_scr[...]
    for (p0, p1), s_part in zip(parts, scores):
        blocks = []
        s_max = None
        for j in range((p1 - p0) * SUBLANES):
            row = p0 * SUBLANES + j
            sj = (s_part[:, j * LANES:(j + 1) * LANES] + base) + g_past[row:row + 1, :]
            blocks.append(sj)
            s_max = sj if s_max is None else jnp.maximum(s_max, sj)
        m_new = jnp.maximum(m_run, jnp.max(s_max, axis=-1, keepdims=True))
        alpha = jnp.exp(m_run - m_new)
        p_sum = None
        p_blocks = []
        for sj in blocks:
            pj = jnp.exp(sj - m_new)
            p_sum = pj if p_sum is None else p_sum + pj
            p_blocks.append(pj.astype(BF16))
        l_run = alpha * l_run + jnp.sum(p_sum, axis=-1, keepdims=True)
        acc_scr[...] = alpha * acc_scr[...] + _dot(
            jnp.concatenate(p_blocks, axis=1),
            vbuf[slot, p0:p1].reshape((p1 - p0) * pg_rows, LANES).astype(BF16))
        m_run = m_new
    m_scr[...] = m_run
    l_scr[...] = l_run

    @pl.when(grp == n_groups - 1)
    def _():
        c_new = jnp.sum(mc_ref[...] * lfc_ref[0], axis=0, keepdims=True)
        s = (_dot_nt(qb, kn_ref[0].astype(BF16)) * scale + rowc) - c_new + nb_ref[...]
        m_old = m_scr[...]
        m_new = jnp.maximum(m_old, jnp.max(s, axis=-1, keepdims=True))
        alpha = jnp.exp(m_old - m_new)
        pr = jnp.exp(s - m_new)
        l_fin = alpha * l_scr[...] + jnp.sum(pr, axis=-1, keepdims=True)
        acc = alpha * acc_scr[...] + _dot(pr.astype(BF16), vn_ref[0].astype(BF16))
        o_ref[0] = (acc / l_fin) * _silu(dg_ref[0])


def _attn_sample(q_hq, kn, vn, lf_row, lf_col, dg_hq, cache_k, cache_v, cache_lf, page_table, *, layer_slot):
    bd, r, dh = q_hq.shape
    n_pages = page_table.shape[1]
    n_new = r // D_HEADS
    n_slots, n_phys = cache_k.shape[:2]
    pg_rows = PAGE_SIZE * D_HEADS
    ck = cache_k.reshape(n_slots, n_phys, pg_rows, dh)
    cv = cache_v.reshape(n_slots, n_phys, pg_rows, dh)
    clf = cache_lf.reshape(n_slots, n_phys, SUBLANES, LANES)
    g_pages = min(PAGES_PER_STEP, n_pages)
    assert n_pages % g_pages == 0
    n_groups = n_pages // g_pages
    tabs = [jnp.asarray(t) for t in _attn_sample_tables(n_new, g_pages)]

    def per_b(shape):
        return pl.BlockSpec((1,) + shape, lambda s, pt: (s // n_groups, 0, 0))

    def const(t):
        return pl.BlockSpec(t.shape, lambda s, pt: (0, 0))

    hbm = pl.BlockSpec(memory_space=pl.ANY)
    grid_spec = pltpu.PrefetchScalarGridSpec(
        num_scalar_prefetch=1,
        grid=(bd * n_groups,),
        in_specs=[per_b((r, dh)), per_b((r, dh)), per_b((r, dh)), per_b((1, r)), per_b((r, 1)), per_b((r, dh))]
        + [const(t) for t in tabs] + [hbm, hbm, hbm],
        out_specs=per_b((r, dh)),
        scratch_shapes=[
            pltpu.VMEM((2, g_pages, pg_rows, dh), F32), pltpu.VMEM((2, g_pages, pg_rows, dh), F32),
            pltpu.VMEM((2, g_pages, SUBLANES, LANES), F32), pltpu.SemaphoreType.DMA((3, 2)),
            pltpu.VMEM((r, 1), F32), pltpu.VMEM((r, 1), F32), pltpu.VMEM((r, dh), F32),
            pltpu.VMEM((1, LANES), F32), pltpu.VMEM((r, LANES), F32)],
    )
    return pl.pallas_call(
        functools.partial(_attn_sample_kernel, layer_slot=layer_slot, n_groups=n_groups, g_pages=g_pages),
        grid_spec=grid_spec,
        out_shape=jax.ShapeDtypeStruct((bd, r, dh), F32),
        compiler_params=_cparams("arbitrary"),
        name="attn_sample",
    )(page_table, q_hq, kn, vn, lf_row, lf_col, dg_hq, *tabs, ck, cv, clf)


TM_NORM = 1024
TM_IN = 2048
TN_IN_EVEN = 512
TN_IN_ODD = 256
TM_OUT = 512
TQ_POOL = 512
TQ_ATTN = 256


def kernel(x_prompt, x_sample, c_prompt, c_sample, state_ret, state_pool, cache_k, cache_v, cache_logf,
           page_table, g_pre, g_post, w_ada, b_ada, w_in_even, w_out_even, ln_a_g, ln_a_b, w_s, b_s,
           ret_g, w_in_odd, b_f, w_out_odd, w_pool, pool_scale):
    bp, seq, d = x_prompt.shape
    bd, n_new, _ = x_sample.shape
    depth = g_pre.shape[0]
    n_odd = w_in_odd.shape[0]
    w = d // 2
    n_past = page_table.shape[1] * PAGE_SIZE
    rows_s = bd * n_new

    c_all = jnp.concatenate([c_prompt, c_sample], axis=0)
    c_all = jnp.pad(c_all, ((0, -c_all.shape[0] % (2 * SUBLANES)), (0, 0)))
    mod = _ada_mod(c_all, w_ada, b_ada)

    def mods(l):
        mp = mod[l, :bp].reshape(bp, 1, 3 * d)
        ms = jnp.repeat(mod[l, bp:bp + bd], n_new, axis=0).reshape(1, rows_s, 3 * d)
        return [(m[..., :d], m[..., d:2 * d], m[..., 2 * d:]) for m in (mp, ms)]

    xp = x_prompt.reshape(bp * seq, d)
    xs = x_sample.reshape(rows_s, d)
    ret_p, ret_s, gv_s, pool_p, pool_s, lfp_l, lfs_l = [], [], [], [], [], [], []
    kv_p = kv_s = None
    zeros_state = jnp.zeros((1, bp) + state_ret.shape[2:], F32)
    w_out_even_bf, w_out_odd_bf, w_pool_bf = w_out_even.astype(BF16), w_out_odd.astype(BF16), w_pool.astype(BF16)
    w_in_odd_t = jnp.swapaxes(w_in_odd, 1, 2)
    all_mods = [mods(l) for l in range(depth)]
    (shift_p, scale_p, _), (shift_s, scale_s, _) = all_mods[0]
    g0 = g_pre[0].reshape(1, d)
    hp = _modnorm_call(xp, scale_p, shift_p, g0, per_row=False, rows_per_mod=seq, tm=TM_NORM)
    hs = _modnorm_call(xs, scale_s, shift_s, g0, per_row=True, rows_per_mod=1, tm=rows_s)
    for l in range(depth):
        j = l // 2
        (_, _, gate_p), (_, _, gate_s) = all_mods[l]
        gpost = g_post[l].reshape(1, d)
        if l % 2 == 0:
            w_out = w_out_even_bf
            zp, zs = _inproj_even(hp, hs, w_in_even, layer_slot=j, tm=TM_IN, tn=TN_IN_EVEN)
            mix = functools.partial(_even_mixer, ln_g=ln_a_g[j], ln_b=ln_a_b[j], w_s=w_s[j], b_s=b_s[j],
                                    ret_g=ret_g[j])
            ap, rp, sp = mix(zp, zeros_state, s0_slot=0, n_batch=bp, seq=seq, pos0=0, emit_vn=False,
                             out_dtype=BF16)
            as_, rs, ss, vn_s = mix(zs, state_ret, s0_slot=j, n_batch=bd, seq=n_new, pos0=n_past, emit_vn=True,
                                    out_dtype=F32)
            ret_p.append(sp)
            ret_s.append(ss)
            gv_s.append(vn_s.reshape(bd, n_new, w))
        else:
            wf = jnp.pad(w_in_odd_t[j, 6 * w:, :], ((0, LANES - D_HEADS), (0, 0)))
            bf = jnp.pad(b_f[j], (0, LANES - D_HEADS)).reshape(1, LANES)
            w_out = w_out_odd_bf
            wp = w_pool_bf
            kv_prev = None if kv_p is None else kv_p + kv_s
            zp, kp_all, vp_all, lfp, zs, ks_all, vs_all, lfs = _inproj_odd(
                hp, hs, w_in_odd_t, wf, bf, kv_prev, layer_slot=j, n_slots=n_odd, tm=TM_IN, tn=TN_IN_ODD)
            kv_p = (kp_all, vp_all)
            kv_s = (ks_all, vs_all)
            fcol, frow = _fcum(lfp, n_batch=bp, seq=seq, blk=TQ_ATTN)
            rp = _attn_prompt(zp, kp_all, vp_all, fcol, frow, layer_slot=j, n_batch=bp, seq=seq, tq=TQ_ATTN,
                              out_dtype=BF16)
            ap = _pool_branch(zp, jnp.zeros((bp, POOL_PREV, w), F32), _pool_counts(0, seq), wp, pool_scale[j],
                              layer_slot=j, n_batch=bp, seq=seq, tq=TQ_POOL, out_dtype=BF16, after=rp)
            pool_p.append(zp.reshape(bp, seq, 4 * w)[:, seq - POOL_BUF:, :w])
            lfp_l.append(lfp[:, :D_HEADS].reshape(bp, seq, D_HEADS))
            prev = jnp.pad(state_pool[j], ((0, 0), (POOL_PREV - POOL_BUF, 0), (0, 0)))
            as_ = _pool_branch(zs, prev, _pool_counts(n_past, POOL_PREV), wp, pool_scale[j],
                               layer_slot=j, n_batch=bd, seq=n_new, tq=n_new, out_dtype=F32)
            zs4 = zs.reshape(bd, n_new, 4, D_HEADS, LANES)
            to_hq = lambda a: a.transpose(0, 2, 1, 3).reshape(bd, D_HEADS * n_new, LANES)
            lf_new = lfs[:, :D_HEADS].reshape(bd, n_new * D_HEADS)
            o_hq = _attn_sample(to_hq(zs4[:, :, 2]), ks_all[j].reshape(bd, n_new * D_HEADS, LANES),
                                vs_all[j].reshape(bd, n_new * D_HEADS, LANES),
                                lf_new.reshape(bd, 1, -1), lf_new.reshape(bd, -1, 1), to_hq(zs4[:, :, 3]),
                                cache_k, cache_v, cache_logf, page_table, layer_slot=j)
            rs = o_hq.reshape(bd, D_HEADS, n_new, LANES).transpose(0, 2, 1, 3).reshape(rows_s, w)
            c_in_s = zs[:, :w].reshape(bd, n_new, w)
            pool_s.append(jnp.concatenate([state_pool[j], c_in_s], axis=1)[:, -POOL_BUF:])
            lfs_l.append(lfs[:, :D_HEADS].reshape(bd, n_new, D_HEADS))
        next_mod = None
        if l + 1 < depth:
            (shift_p, scale_p, _), (shift_s, scale_s, _) = all_mods[l + 1]
            next_mod = ((scale_p, shift_p), (scale_s, shift_s), g_pre[l + 1].reshape(1, d))
        xp, hp, xs, hs = _outproj((ap, rp, xp, gate_p), (as_, rs, xs, gate_s), gpost, w_out, next_mod,
                                  layer_slot=j, rows_per_mod=seq, tm=TM_OUT)

    dh = LANES
    return (xp.reshape(bp, seq, d), xs.reshape(bd, n_new, d),
            jnp.stack(ret_p), jnp.stack(ret_s), jnp.stack(gv_s), jnp.stack(pool_p), jnp.stack(pool_s),
            kv_p[0].reshape(n_odd, bp, seq, D_HEADS, dh), kv_p[1].reshape(n_odd, bp, seq, D_HEADS, dh),
            jnp.stack(lfp_l),
            kv_s[0].reshape(n_odd, bd, n_new, D_HEADS, dh), kv_s[1].reshape(n_odd, bd, n_new, D_HEADS, dh),
            jnp.stack(lfs_l))
```

```python
import functools

import numpy as np
import jax
import jax.numpy as jnp
from jax import lax
from jax.experimental import pallas as pl
from jax.experimental.pallas import tpu as pltpu

F32 = jnp.float32
BF16 = jnp.bfloat16

EPS = 1e-6
ROPE_BASE = 10000.0
CHUNK = 128
A_GROUPS = 8
B_HEADS = 4
POOL_WINDOWS = (2, 4, 8, 16)
POOL_BUF = 15
D_HEADS = 8
PAGE_SIZE = 128
LANES = 128
SUBLANES = 8
NEG_BIG = -1e30
VMEM_LIMIT = 56 * 1024 * 1024


def _cparams(*sem):
    return pltpu.CompilerParams(dimension_semantics=sem, vmem_limit_bytes=VMEM_LIMIT)


def _silu(x):
    return x * jax.nn.sigmoid(x)


def _log_sigmoid(x):
    return jnp.minimum(x, 0.0) - jnp.log1p(jnp.exp(-jnp.abs(x)))


def _dot(a, b):
    return jnp.dot(a, b, preferred_element_type=F32)


def _dot_nt(a, b):
    return lax.dot_general(a, b, (((1,), (1,)), ((), ())), preferred_element_type=F32)


def _dot_tn(a, b):
    return lax.dot_general(a, b, (((0,), (0,)), ((), ())), preferred_element_type=F32)


def _ada_kernel(c_ref, w_ref, b_ref, o_ref):
    a = _silu(c_ref[...]).astype(BF16)
    o_ref[0] = _dot(a, w_ref[0].astype(BF16)) + b_ref[0]


def _ada_mod(c_all, w_ada, b_ada, tn=2048):
    depth, d, n = w_ada.shape
    r = c_all.shape[0]
    return pl.pallas_call(
        _ada_kernel,
        grid=(depth, n // tn),
        in_specs=[
            pl.BlockSpec((r, d), lambda l, j: (0, 0)),
            pl.BlockSpec((1, d, tn), lambda l, j: (l, 0, j)),
            pl.BlockSpec((1, 1, tn), lambda l, j: (l, 0, j)),
        ],
        out_specs=pl.BlockSpec((1, r, tn), lambda l, j: (l, 0, j)),
        out_shape=jax.ShapeDtypeStruct((depth, r, n), F32),
        compiler_params=_cparams("arbitrary", "arbitrary"),
        name="ada_mod",
    )(c_all, w_ada, b_ada.reshape(depth, 1, n))


def _modnorm(x, scale, shift, g):
    y = x * lax.rsqrt(jnp.mean(x * x, axis=-1, keepdims=True) + EPS)
    return (y * (g * (1.0 + scale)) + shift).astype(BF16)


def _modnorm_kernel(x_ref, sc_ref, sh_ref, g_ref, h_ref):
    h_ref[...] = _modnorm(x_ref[...], sc_ref[0], sh_ref[0], g_ref[...])


def _mod_spec(per_row, rows_per_mod, tm, d):
    if per_row:
        return pl.BlockSpec((1, tm, d), lambda i: (0, i, 0))
    return pl.BlockSpec((1, 1, d), lambda i: ((i * tm) // rows_per_mod, 0, 0))


def _modnorm_call(x2d, scale, shift, g, *, per_row, rows_per_mod, tm):
    rows, d = x2d.shape
    return pl.pallas_call(
        _modnorm_kernel,
        grid=(rows // tm,),
        in_specs=[pl.BlockSpec((tm, d), lambda i: (i, 0)),
                  _mod_spec(per_row, rows_per_mod, tm, d), _mod_spec(per_row, rows_per_mod, tm, d),
                  pl.BlockSpec((1, d), lambda i: (0, 0))],
        out_specs=pl.BlockSpec((tm, d), lambda i: (i, 0)),
        out_shape=jax.ShapeDtypeStruct((rows, d), BF16),
        compiler_params=_cparams("arbitrary"),
        name="modnorm",
    )(x2d, scale, shift, g)


def _inproj_even_kernel(h_ref, hs_ref, w_ref, z_ref, zs_ref):
    wb = w_ref[0].astype(BF16)
    z_ref[...] = _dot(h_ref[...], wb)

    @pl.when(pl.program_id(0) == 0)
    def _():
        zs_ref[...] = _dot(hs_ref[...], wb)


def _inproj_odd_kernel(h_ref, hs_ref, w_ref, wf_ref, bf_ref, *rest, tiles_per_group, kv_slot):
    z_ref, k_ref, v_ref, lf_ref, zs_ref, ks_ref, vs_ref, lfs_ref = rest[-8:]
    j = pl.program_id(1)
    group = j // tiles_per_group

    def emit(x_ref, z_o, k_o, v_o, lf_o):
        def project():
            return _dot_nt(x_ref[...], w_ref[0].astype(BF16))

        def put(kv_ref):
            for s in range(kv_ref.shape[0]):
                kv_ref[s] = project() if s == kv_slot else jnp.zeros(kv_ref.shape[1:], F32)

        @pl.when(j == 0)
        def _():
            lf_o[...] = _log_sigmoid(_dot_nt(x_ref[...], wf_ref[...].astype(BF16)) + bf_ref[...])

        @pl.when(jnp.logical_or(group < 3, group == 5))
        def _():
            z_o[...] = project()

        @pl.when(group == 3)
        def _():
            put(k_o)

        @pl.when(group == 4)
        def _():
            put(v_o)

    emit(h_ref, z_ref, k_ref, v_ref, lf_ref)

    @pl.when(pl.program_id(0) == 0)
    def _():
        emit(hs_ref, zs_ref, ks_ref, vs_ref, lfs_ref)


def _inproj_even(h, hs, w_all, *, layer_slot, tm, tn):
    rows, d = h.shape
    rows_s = hs.shape[0]
    n = w_all.shape[2]
    n_j = n // tn
    return pl.pallas_call(
        _inproj_even_kernel,
        grid=(rows // tm, n_j),
        in_specs=[
            pl.BlockSpec((tm, d), lambda i, j: (i, 0)),
            pl.BlockSpec((rows_s, d), lambda i, j: (0, 0)),
            pl.BlockSpec((1, d, tn), lambda i, j: (layer_slot, 0, j)),
        ],
        out_specs=[pl.BlockSpec((tm, tn), lambda i, j: (i, j)),
                   pl.BlockSpec((rows_s, tn), lambda i, j: (0, jnp.where(i == 0, j, n_j - 1)))],
        out_shape=[jax.ShapeDtypeStruct((rows, n), F32), jax.ShapeDtypeStruct((rows_s, n), F32)],
        compiler_params=_cparams("arbitrary", "arbitrary"),
        name="inproj_even",
    )(h, hs, w_all)


def _inproj_odd(h, hs, w_all_t, wf, bf, kv_prev, *, layer_slot, n_slots, tm, tn):
    rows, d = h.shape
    rows_s = hs.shape[0]
    w = (w_all_t.shape[1] - D_HEADS) // 6
    tpg = w // tn
    n_tiles = 6 * tpg

    creates_kv = kv_prev is None
    kv_slots = n_slots if creates_kv else 1
    kv_first = 0 if creates_kv else layer_slot

    def zcol(j):
        return j - jnp.clip(j - (3 * tpg - 1), 0, 2 * tpg)

    def kcol(j):
        return jnp.clip(j - 3 * tpg, 0, tpg - 1)

    def vcol(j):
        return jnp.clip(j - 4 * tpg, 0, tpg - 1)

    def sample_j(i, j):
        return jnp.where(i == 0, j, n_tiles - 1)

    in_specs = [
        pl.BlockSpec((tm, d), lambda i, j: (i, 0)),
        pl.BlockSpec((rows_s, d), lambda i, j: (0, 0)),
        pl.BlockSpec((1, tn, d), lambda i, j: (layer_slot, j, 0)),
        pl.BlockSpec((LANES, d), lambda i, j: (0, 0)),
        pl.BlockSpec((1, LANES), lambda i, j: (0, 0)),
    ]
    args = [h, hs, w_all_t, wf, bf]
    aliases = {}
    if kv_prev is not None:
        in_specs += [pl.BlockSpec(memory_space=pl.ANY)] * 4
        aliases = {len(args): 1, len(args) + 1: 2, len(args) + 2: 5, len(args) + 3: 6}
        args += list(kv_prev)
    return pl.pallas_call(
        functools.partial(_inproj_odd_kernel, tiles_per_group=tpg, kv_slot=layer_slot if creates_kv else 0),
        grid=(rows // tm, n_tiles),
        in_specs=in_specs,
        out_specs=[
            pl.BlockSpec((tm, tn), lambda i, j: (i, zcol(j))),
            pl.BlockSpec((kv_slots, tm, tn), lambda i, j: (kv_first, i, kcol(j))),
            pl.BlockSpec((kv_slots, tm, tn), lambda i, j: (kv_first, i, vcol(j))),
            pl.BlockSpec((tm, LANES), lambda i, j: (i, 0)),
            pl.BlockSpec((rows_s, tn), lambda i, j: (0, zcol(sample_j(i, j)))),
            pl.BlockSpec((kv_slots, rows_s, tn), lambda i, j: (kv_first, 0, kcol(sample_j(i, j)))),
            pl.BlockSpec((kv_slots, rows_s, tn), lambda i, j: (kv_first, 0, vcol(sample_j(i, j)))),
            pl.BlockSpec((rows_s, LANES), lambda i, j: (0, 0)),
        ],
        out_shape=[
            jax.ShapeDtypeStruct((rows, 4 * w), F32),
            jax.ShapeDtypeStruct((n_slots, rows, w), F32),
            jax.ShapeDtypeStruct((n_slots, rows, w), F32),
            jax.ShapeDtypeStruct((rows, LANES), F32),
            jax.ShapeDtypeStruct((rows_s, 4 * w), F32),
            jax.ShapeDtypeStruct((n_slots, rows_s, w), F32),
            jax.ShapeDtypeStruct((n_slots, rows_s, w), F32),
            jax.ShapeDtypeStruct((rows_s, LANES), F32),
        ],
        input_output_aliases=aliases,
        compiler_params=_cparams("arbitrary", "arbitrary"),
        name="inproj_odd",
    )(*args)


def _outproj_kernel(a_ref, b_ref, x_ref, gate_ref, as_ref, bs_ref, xs_ref, gates_ref, g_ref, w1_ref, w2_ref,
                    *rest, with_next):
    if with_next:
        sc_ref, sh_ref, scs_ref, shs_ref, gn_ref, o_ref, h_ref, os_ref, hs_ref = rest
    else:
        o_ref, os_ref = rest

    def finish(a_r, b_r, x_r, gate_r, o_r, next_refs):
        y = _dot(a_r[...].astype(BF16), w1_ref[0]) + _dot(b_r[...].astype(BF16), w2_ref[0])
        yn = y * lax.rsqrt(jnp.mean(y * y, axis=-1, keepdims=True) + EPS)
        x_new = x_r[...] + yn * (gate_r[0] * g_ref[...])
        if next_refs is not None:
            sc_r, sh_r, h_r = next_refs
            h_r[...] = _modnorm(x_new, sc_r[0], sh_r[0], gn_ref[...])
        o_r[...] = x_new

    finish(a_ref, b_ref, x_ref, gate_ref, o_ref, (sc_ref, sh_ref, h_ref) if with_next else None)

    @pl.when(pl.program_id(0) == 0)
    def _():
        finish(as_ref, bs_ref, xs_ref, gates_ref, os_ref, (scs_ref, shs_ref, hs_ref) if with_next else None)


def _outproj(prompt, sample, g, w_bf, next_mod, *, layer_slot, rows_per_mod, tm):
    a, b, x2d, gate = prompt
    a_s, b_s, xs2d, gate_s = sample
    rows, d = x2d.shape
    rows_s = xs2d.shape[0]
    w = a.shape[1]
    mod_spec = _mod_spec(False, rows_per_mod, tm, d)
    mod_s_spec = pl.BlockSpec((1, rows_s, d), lambda i: (0, 0, 0))
    row_spec = pl.BlockSpec((tm, d), lambda i: (i, 0))
    row_s_spec = pl.BlockSpec((rows_s, d), lambda i: (0, 0))
    half_s_spec = pl.BlockSpec((rows_s, w), lambda i: (0, 0))
    vec_spec = pl.BlockSpec((1, d), lambda i: (0, 0))
    in_specs = [
        pl.BlockSpec((tm, w), lambda i: (i, 0)),
        pl.BlockSpec((tm, w), lambda i: (i, 0)),
        row_spec, mod_spec,
        half_s_spec, half_s_spec, row_s_spec, mod_s_spec,
        vec_spec,
        pl.BlockSpec((1, w, d), lambda i: (layer_slot, 0, 0)),
        pl.BlockSpec((1, w, d), lambda i: (layer_slot, 1, 0)),
    ]
    args = [a, b, x2d, gate, a_s, b_s, xs2d, gate_s, g, w_bf, w_bf]
    out_specs = [row_spec]
    out_shape = [jax.ShapeDtypeStruct((rows, d), F32)]
    if next_mod is not None:
        (sc_p, sh_p), (sc_s, sh_s), g_next = next_mod
        in_specs += [mod_spec, mod_spec, mod_s_spec, mod_s_spec, vec_spec]
        args += [sc_p, sh_p, sc_s, sh_s, g_next]
        out_specs.append(row_spec)
        out_shape.append(jax.ShapeDtypeStruct((rows, d), BF16))
    out_specs.append(row_s_spec)
    out_shape.append(jax.ShapeDtypeStruct((rows_s, d), F32))
    if next_mod is not None:
        out_specs.append(row_s_spec)
        out_shape.append(jax.ShapeDtypeStruct((rows_s, d), BF16))
    res = pl.pallas_call(
        functools.partial(_outproj_kernel, with_next=next_mod is not None),
        grid=(rows // tm,),
        in_specs=in_specs,
        out_specs=out_specs,
        out_shape=out_shape,
        compiler_params=_cparams("arbitrary"),
        name="outproj",
    )(*args)
    if next_mod is not None:
        xp, hp, xs, hs = res
        return xp, hp, xs, hs
    xp, xs = res
    return xp, None, xs, None


def _retention_tables(c_len, pos0, n_rows):
    lg = np.log(1.0 - 2.0 ** (-5.0 - np.arange(B_HEADS, dtype=np.float64)))
    t = np.arange(CHUNK, dtype=np.float64)
    diff = t[:, None] - t[None, :]
    dmask = np.where(diff >= 0, np.exp(lg[:, None, None] * np.maximum(diff, 0.0)), 0.0).astype(np.float32)
    qdec = np.exp(lg[None, :] * (t + 1.0)[:, None]).astype(np.float32)
    kdec = np.exp(lg[None, :] * np.maximum(c_len - 1.0 - t, 0.0)[:, None]).astype(np.float32)
    cdec = tuple(float(v) for v in np.exp(lg * c_len).astype(np.float32))
    half = LANES
    inv = ROPE_BASE ** (-np.arange(half, dtype=np.float64) / half)
    pos = (pos0 + np.arange(n_rows)).astype(np.float64)
    ang = pos[:, None] * inv[None, :]
    return dmask, qdec, kdec, cdec, ang


MIXER_CHUNKS_PER_STEP = 4


def _even_mixer_kernel(u_ref, v_ref, ga_ref, q_ref, k_ref, vv_ref, gr_ref,
                       lng_ref, lnb_ref, ws_ref, bst_ref, retg_ref, cos_ref, sin_ref,
                       dmask_ref, qdec_ref, kdec_ref, s0_ref, *rest, c_in, cdec, emit_vn):
    if emit_vn:
        ao_ref, ro_ref, so_ref, vn_ref, s_scr = rest
    else:
        ao_ref, ro_ref, so_ref, s_scr = rest
    c = pl.program_id(1)

    @pl.when(c == 0)
    def _():
        s_scr[...] = s0_ref[0, 0]

    n_sub = max(u_ref.shape[0] // CHUNK, 1)
    w = u_ref.shape[1]
    gw = w // A_GROUPS
    dk = w // B_HEADS
    half = dk // 2
    causal = (lax.broadcasted_iota(jnp.int32, (CHUNK, CHUNK), 0)
              >= lax.broadcasted_iota(jnp.int32, (CHUNK, CHUNK), 1))
    wms = [jnp.where(causal, ws_ref[g], 0.0).astype(BF16) for g in range(A_GROUPS)]
    for cc in range(n_sub):
        r0 = cc * CHUNK
        n_out = min(c_in, CHUNK)
        rows = slice(r0, r0 + n_out)

        def ld(ref):
            if c_in >= CHUNK:
                return ref[r0:r0 + CHUNK, :]
            return jnp.concatenate([ref[...], jnp.zeros((CHUNK - c_in, w), F32)], axis=0)

        v = ld(v_ref)
        xc = v - jnp.mean(v, axis=-1, keepdims=True)
        vn = xc * lax.rsqrt(jnp.mean(xc * xc, axis=-1, keepdims=True) + EPS) * lng_ref[...] + lnb_ref[...]
        if emit_vn:
            vn_ref[rows, :] = vn[:n_out]
        vnb = vn.astype(BF16)
        u = ld(u_ref)
        ga = ld(ga_ref)
        for g in range(A_GROUPS):
            sl = slice(g * gw, (g + 1) * gw)
            mixed = _dot(wms[g], vnb[:, sl]) + bst_ref[:, g:g + 1]
            ao_ref[rows, sl] = (u[:, sl] * mixed * _silu(ga[:, sl]))[:n_out].astype(ao_ref.dtype)

        cos = cos_ref[r0:r0 + CHUNK, :]
        sin = sin_ref[r0:r0 + CHUNK, :]
        q = ld(q_ref)
        k = ld(k_ref)
        vv = ld(vv_ref)
        gr = ld(gr_ref)
        for h in range(B_HEADS):
            sl = slice(h * dk, (h + 1) * dk)
            q1, q2 = q[:, h * dk:h * dk + half], q[:, h * dk + half:(h + 1) * dk]
            k1, k2 = k[:, h * dk:h * dk + half], k[:, h * dk + half:(h + 1) * dk]
            qr = jnp.concatenate([q1 * cos - q2 * sin, q1 * sin + q2 * cos], axis=1)
            kr = jnp.concatenate([k1 * cos - k2 * sin, k1 * sin + k2 * cos], axis=1) * (dk ** -0.5)
            qrb = qr.astype(BF16)
            vb = vv[:, sl].astype(BF16)
            inner = _dot_nt(qrb, kr.astype(BF16)) * dmask_ref[h]
            s = s_scr[h]
            o = _dot(inner.astype(BF16), vb) + _dot(qrb, s.astype(BF16)) * qdec_ref[:, h:h + 1]
            kd = (kr * kdec_ref[:, h:h + 1]).astype(BF16)
            s_scr[h] = s * cdec[h] + _dot_tn(kd, vb)
            on = o * lax.rsqrt(jnp.mean(o * o, axis=-1, keepdims=True) + EPS) * retg_ref[:, sl]
            ro_ref[rows, sl] = (on * _silu(gr[:, sl]))[:n_out].astype(ro_ref.dtype)

    @pl.when(c == pl.num_programs(1) - 1)
    def _():
        so_ref[0] = s_scr[...]


def _even_mixer(z, s0_all, ln_g, ln_b, w_s, b_s, ret_g, *, s0_slot, n_batch, seq, pos0, emit_vn, out_dtype):
    rows, n = z.shape
    s_shape = s0_all.shape[2:]
    w = n // 7
    c_len = CHUNK if seq % CHUNK == 0 else seq
    n_chunks_total = seq // min(seq, CHUNK)
    per_step = min(MIXER_CHUNKS_PER_STEP, n_chunks_total)
    assert n_chunks_total % per_step == 0
    c_in = min(seq, CHUNK) * per_step
    n_chunks = n_chunks_total // per_step
    dmask, qdec, kdec, cdec, ang = _retention_tables(c_len, pos0, n_chunks_total * CHUNK)
    cos = jnp.asarray(np.cos(ang).astype(np.float32))
    sin = jnp.asarray(np.sin(ang).astype(np.float32))

    def zspec(col):
        return pl.BlockSpec((c_in, w), lambda b, c, col=col: (b * n_chunks + c, col))

    def const(shape):
        return pl.BlockSpec(shape, lambda b, c: (0,) * len(shape))

    row_spec = pl.BlockSpec((c_in, w), lambda b, c: (b * n_chunks + c, 0))
    state_in_spec = pl.BlockSpec((1, 1) + s_shape, lambda b, c: (s0_slot, b, 0, 0, 0))
    state_spec = pl.BlockSpec((1,) + s_shape, lambda b, c: (b, 0, 0, 0))
    out_specs = [row_spec, row_spec, state_spec]
    out_shape = [jax.ShapeDtypeStruct((rows, w), out_dtype),
                 jax.ShapeDtypeStruct((rows, w), out_dtype),
                 jax.ShapeDtypeStruct((n_batch,) + s_shape, F32)]
    if emit_vn:
        out_specs.append(row_spec)
        out_shape.append(jax.ShapeDtypeStruct((rows, w), F32))
    return pl.pallas_call(
        functools.partial(_even_mixer_kernel, c_in=c_in, cdec=cdec, emit_vn=emit_vn),
        grid=(n_batch, n_chunks),
        in_specs=[zspec(i) for i in range(7)] + [
            const((1, w)), const((1, w)), const(w_s.shape), const((CHUNK, A_GROUPS)), const((1, w)),
            pl.BlockSpec((per_step * CHUNK, LANES), lambda b, c: (c, 0)),
            pl.BlockSpec((per_step * CHUNK, LANES), lambda b, c: (c, 0)),
            const(dmask.shape), const(qdec.shape), const(kdec.shape),
            state_in_spec,
        ],
        out_specs=out_specs,
        out_shape=out_shape,
        scratch_shapes=[pltpu.VMEM(s_shape, F32)],
        compiler_params=_cparams("arbitrary", "arbitrary"),
        name="even_mixer",
    )(z, z, z, z, z, z, z, ln_g.reshape(1, w), ln_b.reshape(1, w), w_s, b_s.T, ret_g.reshape(1, w),
      cos, sin, jnp.asarray(dmask), jnp.asarray(qdec), jnp.asarray(kdec), s0_all)


POOL_PREV = 16
POOL_LEAD = SUBLANES


def _pool_kernel(cin_ref, cg_ref, prev_ref, cnt_ref, wp_ref, ps_ref, *rest, tq, t_pad):
    o_ref, e0, e1, e2, e3, e4 = rest[-6:]
    t = pl.program_id(1)
    base = POOL_LEAD + POOL_PREV
    total = base + t_pad
    w = cin_ref.shape[1]
    gw = w // len(POOL_WINDOWS)

    @pl.when(t == 0)
    def _():
        for e in (e0, e1, e2, e3, e4):
            e[0:POOL_LEAD, :] = jnp.zeros((POOL_LEAD, w), F32)
        e0[POOL_LEAD:base, :] = prev_ref[0]
        if t_pad > tq:
            e0[base + tq:total, :] = jnp.zeros((t_pad - tq, w), F32)

    e0[base:base + tq, :] = cin_ref[...]
    levels = (e0, e1, e2, e3, e4)
    for lvl in range(1, 5):
        shift = 1 << (lvl - 1)
        c0 = (lvl - 1) * gw
        src, dst = levels[lvl - 1], levels[lvl]
        dst[POOL_LEAD:total, c0:] = (src[POOL_LEAD:total, c0:]
                                     + src[POOL_LEAD - shift:total - shift, c0:])
    cur = e0[base:total, :]
    cg = cg_ref[...]
    for gi in range(len(POOL_WINDOWS)):
        sl = slice(gi * gw, (gi + 1) * gw)
        win = levels[gi + 1][base:total, sl]
        pooled = win / cnt_ref[:, gi:gi + 1] - cur[:, sl]
        mixed = _dot(pooled.astype(BF16), wp_ref[0, gi]) * ps_ref[:, sl]
        o_ref[:, sl] = (mixed[:tq] * _silu(cg[:, sl])).astype(o_ref.dtype)
    e0[POOL_LEAD:base, :] = e0[POOL_LEAD + t_pad:base + t_pad, :]


def _pool_branch(z_main, prev, cnt, wp_bf, pool_scale, *, layer_slot, n_batch, seq, tq, out_dtype, after=None):
    rows = z_main.shape[0]
    w = wp_bf.shape[1] * wp_bf.shape[2]
    t_pad = max(tq, POOL_PREV)
    n_t = seq // tq
    ext = pltpu.VMEM((POOL_LEAD + POOL_PREV + t_pad, w), F32)
    in_specs = [
        pl.BlockSpec((tq, w), lambda b, t: (b * n_t + t, 0)),
        pl.BlockSpec((tq, w), lambda b, t: (b * n_t + t, 1)),
        pl.BlockSpec((1, POOL_PREV, w), lambda b, t: (b, 0, 0)),
        pl.BlockSpec((t_pad, len(POOL_WINDOWS)), lambda b, t: (t, 0)),
        pl.BlockSpec((1,) + wp_bf.shape[1:], lambda b, t: (layer_slot, 0, 0, 0)),
        pl.BlockSpec((1, w), lambda b, t: (0, 0)),
    ]
    args = [z_main, z_main, prev, cnt, wp_bf, pool_scale.reshape(1, w)]
    if after is not None:
        in_specs.append(pl.BlockSpec((2 * SUBLANES, LANES), lambda b, t: (0, 0)))
        args.append(after)
    return pl.pallas_call(
        functools.partial(_pool_kernel, tq=tq, t_pad=t_pad),
        grid=(n_batch, n_t),
        in_specs=in_specs,
        out_specs=pl.BlockSpec((tq, w), lambda b, t: (b * n_t + t, 0)),
        out_shape=jax.ShapeDtypeStruct((rows, w), out_dtype),
        scratch_shapes=[ext] * 5,
        compiler_params=_cparams("arbitrary", "arbitrary"),
        name="pool_branch",
    )(*args)


def _pool_counts(pos0, n_rows):
    pos = pos0 + np.arange(n_rows)
    return jnp.asarray(np.stack([np.minimum(pos + 1, wd) for wd in POOL_WINDOWS], axis=1).astype(np.float32))


def _split3(x):
    a = x.astype(BF16)
    r1 = x - a.astype(F32)
    b = r1.astype(BF16)
    c = (r1 - b.astype(F32)).astype(BF16)
    return a, b, c


def _fcum_kernel(lf_ref, fc_ref, fr_ref, *, blk):
    n_blk = lf_ref.shape[0] // blk
    tri = (lax.broadcasted_iota(jnp.int32, (blk, blk), 0)
           >= lax.broadcasted_iota(jnp.int32, (blk, blk), 1)).astype(BF16)
    carry = jnp.zeros((1, LANES), F32)
    for i in range(n_blk):
        a, b, c = _split3(lf_ref[i * blk:(i + 1) * blk, :])
        cs = (_dot(tri, a) + _dot(tri, b)) + _dot(tri, c) + carry
        carry = cs[blk - 1:blk, :]
        fc_ref[i * blk:(i + 1) * blk, :] = cs
        fr_ref[0, i] = cs.T[:D_HEADS, :]


def _fcum(lf, *, n_batch, seq, blk):
    return pl.pallas_call(
        functools.partial(_fcum_kernel, blk=blk),
        grid=(n_batch,),
        in_specs=[pl.BlockSpec((seq, LANES), lambda b: (b, 0))],
        out_specs=[pl.BlockSpec((seq, LANES), lambda b: (b, 0)),
                   pl.BlockSpec((1, seq // blk, D_HEADS, blk), lambda b: (b, 0, 0, 0))],
        out_shape=[jax.ShapeDtypeStruct((n_batch * seq, LANES), F32),
                   jax.ShapeDtypeStruct((n_batch, seq // blk, D_HEADS, blk), F32)],
        compiler_params=_cparams("arbitrary"),
        name="forget_cumsum",
    )(lf)


SCORE_LEAD = 3


def _attn_prompt_kernel(q_ref, dg_ref, k_ref, v_ref, fc_ref, fr_ref, o_ref,
                        kb_scr, vt_scr, qt_scr, *acc_scrs, tq):
    i = pl.program_id(1)
    dh = LANES
    scale = dh ** -0.5
    n_blk = k_ref.shape[1] // tq

    @pl.when(i == 0)
    def _():
        kb_scr[...] = k_ref[0].astype(BF16)
        for h in range(D_HEADS):
            for kb in range(n_blk):
                vt_scr[kb, h * dh:(h + 1) * dh, :] = (
                    v_ref[0, kb * tq:(kb + 1) * tq, h * dh:(h + 1) * dh].T.astype(BF16))

    for h in range(D_HEADS):
        qt_scr[h] = (q_ref[:, h * dh:(h + 1) * dh] * scale).T.astype(BF16)
        acc_scrs[h][...] = jnp.zeros((dh, tq), F32)
    key_le_query = (lax.broadcasted_iota(jnp.int32, (tq, tq), 0)
                    <= lax.broadcasted_iota(jnp.int32, (tq, tq), 1))

    def block(kb, stats, masked):
        r0 = pl.multiple_of(kb * tq, tq)
        new_stats = []

        def scores(h):
            sl = slice(h * dh, (h + 1) * dh)
            return _dot(kb_scr[pl.ds(r0, tq), sl], qt_scr[h]) - fc_ref[pl.ds(r0, tq), h:h + 1]

        pending = [scores(h) for h in range(SCORE_LEAD)]
        for h in range(D_HEADS):
            sl = slice(h * dh, (h + 1) * dh)
            m_old, l_old = stats[h]
            fq = fr_ref[0, 0, h:h + 1, :]
            u = pending.pop(0)
            if h + SCORE_LEAD < D_HEADS:
                pending.append(scores(h + SCORE_LEAD))
            if masked:
                u = jnp.where(key_le_query, u, NEG_BIG)
            m_new = jnp.maximum(m_old, jnp.max(u, axis=0, keepdims=True) + fq)
            alpha = jnp.exp(m_old - m_new)
            p = jnp.exp(u - (m_new - fq))
            new_stats.append((m_new, alpha * l_old + jnp.sum(p, axis=0, keepdims=True)))
            acc_scrs[h][...] = alpha * acc_scrs[h][...] + _dot(vt_scr[kb, sl, :], p.astype(BF16))
        return tuple(new_stats)

    stats0 = tuple((jnp.full((1, tq), NEG_BIG, F32), jnp.zeros((1, tq), F32)) for _ in range(D_HEADS))
    stats = lax.fori_loop(0, i, lambda kb, st: block(kb, st, False), stats0)
    stats = block(i, stats, True)
    for h in range(D_HEADS):
        sl = slice(h * dh, (h + 1) * dh)
        o_ref[:, sl] = ((acc_scrs[h][...] / stats[h][1]).T * _silu(dg_ref[:, sl])).astype(o_ref.dtype)


def _attn_prompt(z_main, k_all, v_all, fcol, frow, *, layer_slot, n_batch, seq, tq, out_dtype):
    rows = z_main.shape[0]
    w = k_all.shape[2]
    n_q = seq // tq
    return pl.pallas_call(
        functools.partial(_attn_prompt_kernel, tq=tq),
        grid=(n_batch, n_q),
        in_specs=[
            pl.BlockSpec((tq, w), lambda b, i: (b * n_q + i, 2)),
            pl.BlockSpec((tq, w), lambda b, i: (b * n_q + i, 3)),
            pl.BlockSpec((1, seq, w), lambda b, i: (layer_slot, b, 0)),
            pl.BlockSpec((1, seq, w), lambda b, i: (layer_slot, b, 0)),
            pl.BlockSpec((seq, LANES), lambda b, i: (b, 0)),
            pl.BlockSpec((1, 1, D_HEADS, tq), lambda b, i: (b, i, 0, 0)),
        ],
        out_specs=pl.BlockSpec((tq, w), lambda b, i: (b * n_q + i, 0)),
        out_shape=jax.ShapeDtypeStruct((rows, w), out_dtype),
        scratch_shapes=[pltpu.VMEM((seq, w), BF16), pltpu.VMEM((n_q, w, tq), BF16),
                        pltpu.VMEM((D_HEADS, LANES, tq), BF16)]
        + [pltpu.VMEM((LANES, tq), F32)] * D_HEADS,
        compiler_params=_cparams("arbitrary", "arbitrary"),
        name="attn_prompt",
    )(z_main, z_main, k_all, v_all, fcol, frow)


PAGES_PER_STEP = 16
SAMPLE_SPLIT = 4


def _attn_sample_tables(n_new, g_pages):
    pg = np.arange(LANES)
    same_head = (pg[:, None] % D_HEADS) == (pg[None, :] % D_HEADS)
    later = (pg[:, None] // D_HEADS) > (pg[None, :] // D_HEADS)
    c_later = (same_head & later).astype(np.float32)
    c_same = same_head.astype(np.float32)
    rows = np.arange(g_pages * SUBLANES)
    r_later = (rows[None, :] > rows[:, None]).astype(np.float32)
    r = np.arange(D_HEADS * n_new)
    past_ok = (pg[None, :] % D_HEADS) == (r[:, None] // n_new)
    past_bias = np.where(past_ok, 0.0, NEG_BIG).astype(np.float32)
    cn = np.arange(n_new * D_HEADS)
    new_ok = ((cn[None, :] % D_HEADS) == (r[:, None] // n_new)) & ((cn[None, :] // D_HEADS) <= (r[:, None] % n_new))
    new_bias = np.where(new_ok, 0.0, NEG_BIG).astype(np.float32)
    m_cols = ((cn[:, None] % D_HEADS == cn[None, :] % D_HEADS)
              & (cn[:, None] // D_HEADS <= cn[None, :] // D_HEADS)).astype(np.float32)
    m_rows = ((cn[None, :] % D_HEADS == r[:, None] // n_new)
              & (cn[None, :] // D_HEADS <= r[:, None] % n_new)).astype(np.float32)
    return c_later, c_same, r_later, past_bias, new_bias, m_cols, m_rows


def _page_copies(pt_ref, ck_hbm, cv_hbm, clf_hbm, kbuf, vbuf, lfbuf, sems, step, slot, *,
                 layer_slot, n_groups, g_pages):
    b = step // n_groups
    first_page = (n_groups - 1 - step % n_groups) * g_pages
    copies = []
    for g in range(g_pages):
        page = pt_ref[b, first_page + g]
        copies.append(pltpu.make_async_copy(ck_hbm.at[layer_slot, page], kbuf.at[slot, g], sems.at[0, slot]))
        copies.append(pltpu.make_async_copy(cv_hbm.at[layer_slot, page], vbuf.at[slot, g], sems.at[1, slot]))
        copies.append(pltpu.make_async_copy(clf_hbm.at[layer_slot, page], lfbuf.at[slot, g], sems.at[2, slot]))
    return copies


def _attn_sample_kernel(pt_ref, q_ref, kn_ref, vn_ref, lfr_ref, lfc_ref, dg_ref,
                        cl_ref, cs_ref, rl_ref, pb_ref, nb_ref, mc_ref, mr_ref,
                        ck_hbm, cv_hbm, clf_hbm, o_ref,
                        kbuf, vbuf, lfbuf, sems, m_scr, l_scr, acc_scr, run_scr, base_scr, *,
                        layer_slot, n_groups, g_pages):
    step = pl.program_id(0)
    n_steps = pl.num_programs(0)
    slot = step % 2
    grp = step % n_groups
    scale = LANES ** -0.5
    hi = lax.Precision.HIGHEST
    copies = functools.partial(_page_copies, pt_ref, ck_hbm, cv_hbm, clf_hbm, kbuf, vbuf, lfbuf, sems,
                               layer_slot=layer_slot, n_groups=n_groups, g_pages=g_pages)

    @pl.when(step == 0)
    def _():
        for c in copies(0, 0):
            c.start()

    @pl.when(step + 1 < n_steps)
    def _():
        for c in copies(step + 1, 1 - slot):
            c.start()

    qb = q_ref[0].astype(BF16)
    rowc = jnp.sum(mr_ref[...] * lfr_ref[0], axis=-1, keepdims=True)

    @pl.when(grp == 0)
    def _():
        m_scr[...] = jnp.full(m_scr.shape, NEG_BIG, F32)
        l_scr[...] = jnp.zeros(l_scr.shape, F32)
        acc_scr[...] = jnp.zeros(acc_scr.shape, F32)
        run_scr[...] = jnp.zeros(run_scr.shape, F32)
        base_scr[...] = rowc + pb_ref[...]

    for c in copies(step, slot):
        c.wait()

    n_rows = g_pages * SUBLANES
    lf = lfbuf[slot].reshape(n_rows, LANES)
    within = jnp.dot(lf, cl_ref[...], precision=hi, preferred_element_type=F32)
    rowtot = jnp.dot(lf, cs_ref[...], precision=hi, preferred_element_type=F32)
    later_rows = jnp.dot(rl_ref[...], rowtot, precision=hi, preferred_element_type=F32)
    g_past = (within + later_rows) + run_scr[...]
    run_scr[...] = run_scr[...] + jnp.sum(rowtot, axis=0, keepdims=True)

    pg_rows = PAGE_SIZE * D_HEADS
    n_part = max(g_pages // SAMPLE_SPLIT, 1)
    parts = [(p0, min(p0 + n_part, g_pages)) for p0 in range(0, g_pages, n_part)]
    scores = [_dot_nt(qb, kbuf[slot, p0:p1].reshape((p1 - p0) * pg_rows, LANES).astype(BF16)) * scale
              for p0, p1 in parts]
    base = base_scr[...]
    m_run, l_run = m_scr[...], l_scr[...]
    for (p0, p1), s_part in zip(parts, scores):
        blocks = []
        s_max = None
        for j in range((p1 - p0) * SUBLANES):
            row = p0 * SUBLANES + j
            sj = (s_part[:, j * LANES:(j + 1) * LANES] + base) + g_past[row:row + 1, :]
            blocks.append(sj)
            s_max = sj if s_max is None else jnp.maximum(s_max, sj)
        m_new = jnp.maximum(m_run, jnp.max(s_max, axis=-1, keepdims=True))
        alpha = jnp.exp(m_run - m_new)
        p_sum = None
        p_blocks = []
        for sj in blocks:
            pj = jnp.exp(sj - m_new)
            p_sum = pj if p_sum is None else p_sum + pj
            p_blocks.append(pj.astype(BF16))
        l_run = alpha * l_run + jnp.sum(p_sum, axis=-1, keepdims=True)
        acc_scr[...] = alpha * acc_scr[...] + _dot(
            jnp.concatenate(p_blocks, axis=1),
            vbuf[slot, p0:p1].reshape((p1 - p0) * pg_rows, LANES).astype(BF16))
        m_run = m_new
    m_scr[...] = m_run
    l_scr[...] = l_run

    @pl.when(grp == n_groups - 1)
    def _():
        c_new = jnp.sum(mc_ref[...] * lfc_ref[0], axis=0, keepdims=True)
        s = (_dot_nt(qb, kn_ref[0].astype(BF16)) * scale + rowc) - c_new + nb_ref[...]
        m_old = m_scr[...]
        m_new = jnp.maximum(m_old, jnp.max(s, axis=-1, keepdims=True))
        alpha = jnp.exp(m_old - m_new)
        pr = jnp.exp(s - m_new)
        l_fin = alpha * l_scr[...] + jnp.sum(pr, axis=-1, keepdims=True)
        acc = alpha * acc_scr[...] + _dot(pr.astype(BF16), vn_ref[0].astype(BF16))
        o_ref[0] = (acc / l_fin) * _silu(dg_ref[0])


def _attn_sample(q_hq, kn, vn, lf_row, lf_col, dg_hq, cache_k, cache_v, cache_lf, page_table, *, layer_slot):
    bd, r, dh = q_hq.shape
    n_pages = page_table.shape[1]
    n_new = r // D_HEADS
    n_slots, n_phys = cache_k.shape[:2]
    pg_rows = PAGE_SIZE * D_HEADS
    ck = cache_k.reshape(n_slots, n_phys, pg_rows, dh)
    cv = cache_v.reshape(n_slots, n_phys, pg_rows, dh)
    clf = cache_lf.reshape(n_slots, n_phys, SUBLANES, LANES)
    g_pages = min(PAGES_PER_STEP, n_pages)
    assert n_pages % g_pages == 0
    n_groups = n_pages // g_pages
    tabs = [jnp.asarray(t) for t in _attn_sample_tables(n_new, g_pages)]

    def per_b(shape):
        return pl.BlockSpec((1,) + shape, lambda s, pt: (s // n_groups, 0, 0))

    def const(t):
        return pl.BlockSpec(t.shape, lambda s, pt: (0, 0))

    hbm = pl.BlockSpec(memory_space=pl.ANY)
    grid_spec = pltpu.PrefetchScalarGridSpec(
        num_scalar_prefetch=1,
        grid=(bd * n_groups,),
        in_specs=[per_b((r, dh)), per_b((r, dh)), per_b((r, dh)), per_b((1, r)), per_b((r, 1)), per_b((r, dh))]
        + [const(t) for t in tabs] + [hbm, hbm, hbm],
        out_specs=per_b((r, dh)),
        scratch_shapes=[
            pltpu.VMEM((2, g_pages, pg_rows, dh), F32), pltpu.VMEM((2, g_pages, pg_rows, dh), F32),
            pltpu.VMEM((2, g_pages, SUBLANES, LANES), F32), pltpu.SemaphoreType.DMA((3, 2)),
            pltpu.VMEM((r, 1), F32), pltpu.VMEM((r, 1), F32), pltpu.VMEM((r, dh), F32),
            pltpu.VMEM((1, LANES), F32), pltpu.VMEM((r, LANES), F32)],
    )
    return pl.pallas_call(
        functools.partial(_attn_sample_kernel, layer_slot=layer_slot, n_groups=n_groups, g_pages=g_pages),
        grid_spec=grid_spec,
        out_shape=jax.ShapeDtypeStruct((bd, r, dh), F32),
        compiler_params=_cparams("arbitrary"),
        name="attn_sample",
    )(page_table, q_hq, kn, vn, lf_row, lf_col, dg_hq, *tabs, ck, cv, clf)


TM_NORM = 1024
TM_IN = 2048
TN_IN_EVEN = 512
TN_IN_ODD = 256
TM_OUT = 512
TQ_POOL = 1024
TQ_ATTN = 256


def kernel(x_prompt, x_sample, c_prompt, c_sample, state_ret, state_pool, cache_k, cache_v, cache_logf,
           page_table, g_pre, g_post, w_ada, b_ada, w_in_even, w_out_even, ln_a_g, ln_a_b, w_s, b_s,
           ret_g, w_in_odd, b_f, w_out_odd, w_pool, pool_scale):
    bp, seq, d = x_prompt.shape
    bd, n_new, _ = x_sample.shape
    depth = g_pre.shape[0]
    n_odd = w_in_odd.shape[0]
    w = d // 2
    n_past = page_table.shape[1] * PAGE_SIZE
    rows_s = bd * n_new

    c_all = jnp.concatenate([c_prompt, c_sample], axis=0)
    c_all = jnp.pad(c_all, ((0, -c_all.shape[0] % (2 * SUBLANES)), (0, 0)))
    mod = _ada_mod(c_all, w_ada, b_ada)

    def mods(l):
        mp = mod[l, :bp].reshape(bp, 1, 3 * d)
        ms = jnp.repeat(mod[l, bp:bp + bd], n_new, axis=0).reshape(1, rows_s, 3 * d)
        return [(m[..., :d], m[..., d:2 * d], m[..., 2 * d:]) for m in (mp, ms)]

    xp = x_prompt.reshape(bp * seq, d)
    xs = x_sample.reshape(rows_s, d)
    ret_p, ret_s, gv_s, pool_p, pool_s, lfp_l, lfs_l = [], [], [], [], [], [], []
    kv_p = kv_s = None
    zeros_state = jnp.zeros((1, bp) + state_ret.shape[2:], F32)
    w_out_even_bf, w_out_odd_bf, w_pool_bf = w_out_even.astype(BF16), w_out_odd.astype(BF16), w_pool.astype(BF16)
    w_in_odd_t = jnp.swapaxes(w_in_odd, 1, 2)
    all_mods = [mods(l) for l in range(depth)]
    (shift_p, scale_p, _), (shift_s, scale_s, _) = all_mods[0]
    g0 = g_pre[0].reshape(1, d)
    hp = _modnorm_call(xp, scale_p, shift_p, g0, per_row=False, rows_per_mod=seq, tm=TM_NORM)
    hs = _modnorm_call(xs, scale_s, shift_s, g0, per_row=True, rows_per_mod=1, tm=rows_s)
    for l in range(depth):
        j = l // 2
        (_, _, gate_p), (_, _, gate_s) = all_mods[l]
        gpost = g_post[l].reshape(1, d)
        if l % 2 == 0:
            w_out = w_out_even_bf
            zp, zs = _inproj_even(hp, hs, w_in_even, layer_slot=j, tm=TM_IN, tn=TN_IN_EVEN)
            mix = functools.partial(_even_mixer, ln_g=ln_a_g[j], ln_b=ln_a_b[j], w_s=w_s[j], b_s=b_s[j],
                                    ret_g=ret_g[j])
            ap, rp, sp = mix(zp, zeros_state, s0_slot=0, n_batch=bp, seq=seq, pos0=0, emit_vn=False,
                             out_dtype=BF16)
            as_, rs, ss, vn_s = mix(zs, state_ret, s0_slot=j, n_batch=bd, seq=n_new, pos0=n_past, emit_vn=True,
                                    out_dtype=F32)
            ret_p.append(sp)
            ret_s.append(ss)
            gv_s.append(vn_s.reshape(bd, n_new, w))
        else:
            wf = jnp.pad(w_in_odd_t[j, 6 * w:, :], ((0, LANES - D_HEADS), (0, 0)))
            bf = jnp.pad(b_f[j], (0, LANES - D_HEADS)).reshape(1, LANES)
            w_out = w_out_odd_bf
            wp = w_pool_bf
            kv_prev = None if kv_p is None else kv_p + kv_s
            zp, kp_all, vp_all, lfp, zs, ks_all, vs_all, lfs = _inproj_odd(
                hp, hs, w_in_odd_t, wf, bf, kv_prev, layer_slot=j, n_slots=n_odd, tm=TM_IN, tn=TN_IN_ODD)
            kv_p = (kp_all, vp_all)
            kv_s = (ks_all, vs_all)
            fcol, frow = _fcum(lfp, n_batch=bp, seq=seq, blk=TQ_ATTN)
            rp = _attn_prompt(zp, kp_all, vp_all, fcol, frow, layer_slot=j, n_batch=bp, seq=seq, tq=TQ_ATTN,
                              out_dtype=BF16)
            ap = _pool_branch(zp, jnp.zeros((bp, POOL_PREV, w), F32), _pool_counts(0, seq), wp, pool_scale[j],
                              layer_slot=j, n_batch=bp, seq=seq, tq=TQ_POOL, out_dtype=BF16, after=rp)
            pool_p.append(zp.reshape(bp, seq, 4 * w)[:, seq - POOL_BUF:, :w])
            lfp_l.append(lfp[:, :D_HEADS].reshape(bp, seq, D_HEADS))
            prev = jnp.pad(state_pool[j], ((0, 0), (POOL_PREV - POOL_BUF, 0), (0, 0)))
            as_ = _pool_branch(zs, prev, _pool_counts(n_past, POOL_PREV), wp, pool_scale[j],
                               layer_slot=j, n_batch=bd, seq=n_new, tq=n_new, out_dtype=F32)
            zs4 = zs.reshape(bd, n_new, 4, D_HEADS, LANES)
            to_hq = lambda a: a.transpose(0, 2, 1, 3).reshape(bd, D_HEADS * n_new, LANES)
            lf_new = lfs[:, :D_HEADS].reshape(bd, n_new * D_HEADS)
            o_hq = _attn_sample(to_hq(zs4[:, :, 2]), ks_all[j].reshape(bd, n_new * D_HEADS, LANES),
                                vs_all[j].reshape(bd, n_new * D_HEADS, LANES),
                                lf_new.reshape(bd, 1, -1), lf_new.reshape(bd, -1, 1), to_hq(zs4[:, :, 3]),
                                cache_k, cache_v, cache_logf, page_table, layer_slot=j)
            rs = o_hq.reshape(bd, D_HEADS, n_new, LANES).transpose(0, 2, 1, 3).reshape(rows_s, w)
            c_in_s = zs[:, :w].reshape(bd, n_new, w)
            pool_s.append(jnp.concatenate([state_pool[j], c_in_s], axis=1)[:, -POOL_BUF:])
            lfs_l.append(lfs[:, :D_HEADS].reshape(bd, n_new, D_HEADS))
        next_mod = None
        if l + 1 < depth:
            (shift_p, scale_p, _), (shift_s, scale_s, _) = all_mods[l + 1]
            next_mod = ((scale_p, shift_p), (scale_s, shift_s), g_pre[l + 1].reshape(1, d))
        xp, hp, xs, hs = _outproj((ap, rp, xp, gate_p), (as_, rs, xs, gate_s), gpost, w_out, next_mod,
                                  layer_slot=j, rows_per_mod=seq, tm=TM_OUT)

    dh = LANES
    return (xp.reshape(bp, seq, d), xs.reshape(bd, n_new, d),
            jnp.stack(ret_p), jnp.stack(ret_s), jnp.stack(gv_s), jnp.stack(pool_p), jnp.stack(pool_s),
            kv_p[0].reshape(n_odd, bp, seq, D_HEADS, dh), kv_p[1].reshape(n_odd, bp, seq, D_HEADS, dh),
            jnp.stack(lfp_l),
            kv_s[0].reshape(n_odd, bd, n_new, D_HEADS, dh), kv_s[1].reshape(n_odd, bd, n_new, D_HEADS, dh),
            jnp.stack(lfs_l))
```
